```python
import jax, jax.numpy as jnp
from jax import lax
import numpy as np

D_MODEL = 1024
BATCH = 2
SEQ = 8192
DEPTH = 4
DEC_BATCH = 32
DEC_SEQ = 4
PAST_LEN = 8192
PAGE_SIZE = 128

N_HEADS = 16
HEAD_DIM = D_MODEL // N_HEADS
N_KV_HEADS = 4
GROUP = N_HEADS // N_KV_HEADS
MOBA_BLOCK = 256
MOBA_TOPK = 3
Q_BLOCK = 128
D_FF = 2816
CONV_W = 3
N_MIXERS = 2
RMS_EPS = 1e-6
NEG_INF = -1e30

kernel_name = 'moba_stickbreak_convffn_hybrid_step'


def rmsnorm(x, g):
    x32 = x.astype(jnp.float32)
    y = x32 * lax.rsqrt(jnp.mean(x32 * x32, axis=-1, keepdims=True) + RMS_EPS)
    return (y * g.astype(jnp.float32)).astype(x.dtype)


def alibi_slopes():
    h = jnp.arange(1, N_HEADS + 1, dtype=jnp.float32)
    return jnp.exp2(-8.0 * h / N_HEADS).reshape(N_KV_HEADS, GROUP)


def project_qkv(h, w_qkv):
    b, t = h.shape[0], h.shape[1]
    qkv = h @ w_qkv
    nq = N_HEADS * HEAD_DIM
    nk = N_KV_HEADS * HEAD_DIM
    q = qkv[..., :nq].reshape(b, t, N_KV_HEADS, GROUP, HEAD_DIM)
    k = qkv[..., nq:nq + nk].reshape(b, t, N_KV_HEADS, HEAD_DIM)
    v = qkv[..., nq + nk:].reshape(b, t, N_KV_HEADS, HEAD_DIM)
    return q, k, v


def moba_prepare(k, v):
    b, tk = k.shape[0], k.shape[1]
    nb = tk // MOBA_BLOCK
    kb = k[:, :nb * MOBA_BLOCK].reshape(b, nb, MOBA_BLOCK, N_KV_HEADS, HEAD_DIM).transpose(0, 3, 1, 2, 4)
    vb = v[:, :nb * MOBA_BLOCK].reshape(b, nb, MOBA_BLOCK, N_KV_HEADS, HEAD_DIM).transpose(0, 3, 1, 2, 4)
    kmean = jnp.mean(kb.astype(jnp.float32), axis=3)
    pad = jnp.zeros((b, MOBA_BLOCK, N_KV_HEADS, HEAD_DIM), k.dtype)
    k_pad = jnp.concatenate([k, pad], axis=1)
    v_pad = jnp.concatenate([v, pad.astype(v.dtype)], axis=1)
    return kb, vb, kmean, k_pad, v_pad


def moba_query_block(q, q_pos, prep):
    kb, vb, kmean, k_pad, v_pad = prep
    b, c = q.shape[0], q.shape[1]
    nb = kb.shape[2]
    scale = HEAD_DIM ** -0.5
    slopes = alibi_slopes()
    s0 = (q_pos[0] // MOBA_BLOCK) * MOBA_BLOCK
    own_blk = s0 // MOBA_BLOCK
    ko = lax.dynamic_slice_in_dim(k_pad, s0, MOBA_BLOCK, axis=1)
    vo = lax.dynamic_slice_in_dim(v_pad, s0, MOBA_BLOCK, axis=1)
    dist_o = (q_pos[:, None] - (s0 + jnp.arange(MOBA_BLOCK, dtype=jnp.int32))[None, :]).astype(jnp.float32)
    s_own = jnp.einsum('bcngd,bknd->bcngk', q, ko, preferred_element_type=jnp.float32) * scale
    s_own = s_own - slopes[None, None, :, :, None] * dist_o[None, :, None, None, :]
    s_own = jnp.where((dist_o >= 0)[None, :, None, None, :], s_own, NEG_INF)
    if nb == 0:
        p = jax.nn.softmax(s_own, axis=-1)
        return jnp.einsum('bcngk,bknd->bcngd', p.astype(vo.dtype), vo)
    ksel = min(MOBA_TOPK, nb)
    gate = jnp.einsum('bcngd,bnmd->bcngm', q.astype(jnp.float32), kmean)
    gate = jnp.where(jnp.arange(nb) < own_blk, gate, NEG_INF)
    _, idx = lax.top_k(gate, ksel)
    valid = idx < own_blk
    b_ix = jnp.arange(b)[:, None, None, None, None]
    n_ix = jnp.arange(N_KV_HEADS)[None, None, :, None, None]
    kg = kb[b_ix, n_ix, idx]
    vg = vb[b_ix, n_ix, idx]
    s_sel = jnp.einsum('bcngd,bcngjkd->bcngjk', q, kg, preferred_element_type=jnp.float32) * scale
    kpos = idx[..., None] * MOBA_BLOCK + jnp.arange(MOBA_BLOCK, dtype=jnp.int32)
    dist_s = (q_pos[None, :, None, None, None, None] - kpos).astype(jnp.float32)
    s_sel = s_sel - slopes[None, None, :, :, None, None] * dist_s
    s_sel = jnp.where(valid[..., None], s_sel, NEG_INF)
    s_all = jnp.concatenate([s_sel.reshape(b, c, N_KV_HEADS, GROUP, ksel * MOBA_BLOCK), s_own], axis=-1)
    p = jax.nn.softmax(s_all, axis=-1)
    p_sel = p[..., :ksel * MOBA_BLOCK].reshape(b, c, N_KV_HEADS, GROUP, ksel, MOBA_BLOCK).astype(vg.dtype)
    p_own = p[..., ksel * MOBA_BLOCK:].astype(vo.dtype)
    return (jnp.einsum('bcngjk,bcngjkd->bcngd', p_sel, vg)
            + jnp.einsum('bcngk,bknd->bcngd', p_own, vo))


def stickbreak_query_block(q, q_pos, k, v):
    tk = k.shape[1]
    scale = HEAD_DIM ** -0.5
    z = jnp.einsum('bcngd,bknd->bcngk', q, k, preferred_element_type=jnp.float32) * scale
    causal = (jnp.arange(tk, dtype=jnp.int32)[None, :] < q_pos[:, None])[None, :, None, None, :]
    log_keep = jnp.where(causal, jax.nn.log_sigmoid(-z), 0.0)
    log_after = lax.cumsum(log_keep, axis=4, reverse=True) - log_keep
    a = jnp.where(causal, jnp.exp(jax.nn.log_sigmoid(z) + log_after), 0.0)
    return jnp.einsum('bcngk,bknd->bcngd', a.astype(v.dtype), v)


def sweep_query_blocks(fn, q):
    b, t = q.shape[0], q.shape[1]
    nc = t // Q_BLOCK
    qc = jnp.moveaxis(q.reshape(b, nc, Q_BLOCK, N_KV_HEADS, GROUP, HEAD_DIM), 1, 0)
    starts = jnp.arange(nc, dtype=jnp.int32) * Q_BLOCK

    def one(args):
        qb, s = args
        return fn(qb, s + jnp.arange(Q_BLOCK, dtype=jnp.int32))

    out = lax.map(one, (qc, starts))
    return jnp.moveaxis(out, 0, 1).reshape(b, t, N_HEADS * HEAD_DIM)


def mix_prompt(layer, q, k, v):
    if layer % N_MIXERS == 0:
        prep = moba_prepare(k, v)
        return sweep_query_blocks(lambda qb, pos: moba_query_block(qb, pos, prep), q)
    return sweep_query_blocks(lambda qb, pos: stickbreak_query_block(qb, pos, k, v), q)


def mix_sample(layer, q, k, v, q_pos):
    b, c = q.shape[0], q.shape[1]
    if layer % N_MIXERS == 0:
        o = moba_query_block(q, q_pos, moba_prepare(k, v))
    else:
        o = stickbreak_query_block(q, q_pos, k, v)
    return o.reshape(b, c, N_HEADS * HEAD_DIM)


def conv_ffn(h, prev, w_up, conv_w, conv_b, w_down):
    u = h @ w_up
    t = u.shape[1]
    ext = jnp.concatenate([prev.astype(u.dtype), u], axis=1)
    c = conv_b + conv_w[0] * ext[:, 0:t] + conv_w[1] * ext[:, 1:t + 1] + conv_w[2] * ext[:, 2:t + 2]
    gate, val = c[..., :D_FF], c[..., D_FF:]
    y = (jax.nn.silu(gate) * val) @ w_down
    return y, ext[:, -(CONV_W - 1):]


def setup_inputs(seed: int = 0) -> dict:
    key = jax.random.key(seed)
    ks = jax.random.split(key, 16)
    n_pages = PAST_LEN // PAGE_SIZE
    n_used = DEC_BATCH * n_pages
    n_pool = n_used + n_used // 4
    qkv_dim = (N_HEADS + 2 * N_KV_HEADS) * HEAD_DIM
    f32 = jnp.float32
    x_prompt = jax.random.normal(ks[0], (BATCH, SEQ, D_MODEL), f32)
    x_sample = jax.random.normal(ks[1], (DEC_BATCH, DEC_SEQ, D_MODEL), f32)
    cache_k = jax.random.normal(ks[2], (DEPTH, n_pool, PAGE_SIZE, N_KV_HEADS, HEAD_DIM), f32)
    cache_v = jax.random.normal(ks[3], (DEPTH, n_pool, PAGE_SIZE, N_KV_HEADS, HEAD_DIM), f32)
    state_conv = jax.random.normal(ks[4], (DEPTH, DEC_BATCH, CONV_W - 1, 2 * D_FF), f32)
    page_table = jax.random.permutation(ks[5], n_pool)[:n_used].reshape(DEC_BATCH, n_pages).astype(jnp.int32)
    g_attn = 1.0 + 0.02 * jax.random.normal(ks[6], (DEPTH, D_MODEL), f32)
    w_qkv = jax.random.normal(ks[7], (DEPTH, D_MODEL, qkv_dim), f32) * D_MODEL ** -0.5
    w_o = jax.random.normal(ks[8], (DEPTH, N_HEADS * HEAD_DIM, D_MODEL), f32) * (N_HEADS * HEAD_DIM) ** -0.5
    g_ffn = 1.0 + 0.02 * jax.random.normal(ks[9], (DEPTH, D_MODEL), f32)
    w_up = jax.random.normal(ks[10], (DEPTH, D_MODEL, 2 * D_FF), f32) * D_MODEL ** -0.5
    conv_w = jax.random.normal(ks[11], (DEPTH, CONV_W, 2 * D_FF), f32) * CONV_W ** -0.5
    conv_b = 0.01 * jax.random.normal(ks[12], (DEPTH, 2 * D_FF), f32)
    w_down = jax.random.normal(ks[13], (DEPTH, D_FF, D_MODEL), f32) * D_FF ** -0.5
    g_final = 1.0 + 0.02 * jax.random.normal(ks[14], (D_MODEL,), f32)
    return {'x_prompt': x_prompt, 'x_sample': x_sample, 'cache_k': cache_k, 'cache_v': cache_v,
            'state_conv': state_conv, 'page_table': page_table, 'g_attn': g_attn, 'w_qkv': w_qkv,
            'w_o': w_o, 'g_ffn': g_ffn, 'w_up': w_up, 'conv_w': conv_w, 'conv_b': conv_b,
            'w_down': w_down, 'g_final': g_final}


def reference(x_prompt, x_sample, cache_k, cache_v, state_conv, page_table, g_attn, w_qkv, w_o,
              g_ffn, w_up, conv_w, conv_b, w_down, g_final):
    xp, xs = x_prompt, x_sample
    dec_b, dec_s = x_sample.shape[0], x_sample.shape[1]
    past = page_table.shape[1] * PAGE_SIZE
    q_pos_s = past + jnp.arange(dec_s, dtype=jnp.int32)
    conv_zero = jnp.zeros((xp.shape[0], CONV_W - 1, 2 * D_FF), xp.dtype)
    kp_l, vp_l, ks_l, vs_l, cp_l, cs_l = [], [], [], [], [], []
    for l in range(DEPTH):
        q, k, v = project_qkv(rmsnorm(xp, g_attn[l]), w_qkv[l])
        xp = xp + mix_prompt(l, q, k, v) @ w_o[l]
        f, cst = conv_ffn(rmsnorm(xp, g_ffn[l]), conv_zero, w_up[l], conv_w[l], conv_b[l], w_down[l])
        xp = xp + f
        kp_l.append(k)
        vp_l.append(v)
        cp_l.append(cst)
        qs, ksn, vsn = project_qkv(rmsnorm(xs, g_attn[l]), w_qkv[l])
        k_past = cache_k[l][page_table].reshape(dec_b, past, N_KV_HEADS, HEAD_DIM).astype(ksn.dtype)
        v_past = cache_v[l][page_table].reshape(dec_b, past, N_KV_HEADS, HEAD_DIM).astype(vsn.dtype)
        k_all = jnp.concatenate([k_past, ksn], axis=1)
        v_all = jnp.concatenate([v_past, vsn], axis=1)
        xs = xs + mix_sample(l, qs, k_all, v_all, q_pos_s) @ w_o[l]
        fs, css = conv_ffn(rmsnorm(xs, g_ffn[l]), state_conv[l], w_up[l], conv_w[l], conv_b[l], w_down[l])
        xs = xs + fs
        ks_l.append(ksn)
        vs_l.append(vsn)
        cs_l.append(css)
    y_prompt = rmsnorm(xp, g_final)
    y_sample = rmsnorm(xs, g_final)
    return (y_prompt, y_sample, jnp.stack(kp_l), jnp.stack(vp_l), jnp.stack(ks_l), jnp.stack(vs_l),
            jnp.stack(cp_l), jnp.stack(cs_l))
```

```python
import functools

import jax
import jax.numpy as jnp
from jax import lax
from jax.experimental import pallas as pl
from jax.experimental.pallas import tpu as pltpu

D_MODEL = 1024
N_HEADS = 16
HEAD_DIM = 64
N_KV_HEADS = 4
GROUP = N_HEADS // N_KV_HEADS
KV_DIM = N_KV_HEADS * HEAD_DIM
MOBA_BLOCK = 256
MOBA_TOPK = 3
D_FF = 2816
CONV_W = 3
PAGE_SIZE = 128
RMS_EPS = 1e-6
NEG_INF = -1e30
SCALE = HEAD_DIM ** -0.5

FF_CHUNK = 256
N_FF_CHUNKS = D_FF // FF_CHUNK
STICK_KB = 128
STICK_DEAD = -110.0
GATE_LANES = 128
PAGES_PER_STEP = 8
VMEM_LIMIT = 56 * 1024 * 1024

_BF16 = jnp.bfloat16
_F32 = jnp.float32


def _nt_dot(a, b):
    return lax.dot_general(a, b, (((1,), (1,)), ((), ())), preferred_element_type=_F32)


def _lane_group(shape):
    return lax.broadcasted_iota(jnp.int32, shape, len(shape) - 1) >> 6


def _qkv_body(x_ref, g_ref, w_ref, q_ref, k_ref, v_ref, kb_ref, vb_ref, *km_ref, tm):
    x = x_ref[...]
    ms = jnp.mean(x * x, axis=-1, keepdims=True)
    h = (x * lax.rsqrt(ms + RMS_EPS)) * g_ref[...]
    qkv = jnp.dot(h.astype(_BF16), w_ref[...], preferred_element_type=_F32)
    for g in range(GROUP):
        q_ref[g] = qkv[:, g * KV_DIM:(g + 1) * KV_DIM]
    k = qkv[:, D_MODEL:D_MODEL + KV_DIM]
    v = qkv[:, D_MODEL + KV_DIM:]
    k_ref[...] = k
    v_ref[...] = v
    kb_ref[...] = k.astype(_BF16)
    vb_ref[...] = v.astype(_BF16)
    if km_ref:
        for j in range(tm // MOBA_BLOCK):
            km_ref[0][j] = jnp.mean(k[j * MOBA_BLOCK:(j + 1) * MOBA_BLOCK], axis=0, keepdims=True)


def _rms_qkv(x, g, w, *, tm, with_kmean):
    t = x.shape[0]
    qkv_dim = w.shape[1]
    out_shape = [
        jax.ShapeDtypeStruct((GROUP, t, KV_DIM), _F32),
        jax.ShapeDtypeStruct((t, KV_DIM), _F32),
        jax.ShapeDtypeStruct((t, KV_DIM), _F32),
        jax.ShapeDtypeStruct((t, KV_DIM), _BF16),
        jax.ShapeDtypeStruct((t, KV_DIM), _BF16),
    ]
    row = pl.BlockSpec((tm, KV_DIM), lambda i: (i, 0))
    out_specs = [pl.BlockSpec((GROUP, tm, KV_DIM), lambda i: (0, i, 0)), row, row, row, row]
    if with_kmean:
        out_shape.append(jax.ShapeDtypeStruct((t // MOBA_BLOCK, 1, KV_DIM), _F32))
        out_specs.append(pl.BlockSpec((tm // MOBA_BLOCK, 1, KV_DIM), lambda i: (i, 0, 0)))
    return pl.pallas_call(
        functools.partial(_qkv_body, tm=tm),
        grid=(t // tm,),
        in_specs=[
            pl.BlockSpec((tm, D_MODEL), lambda i: (i, 0)),
            pl.BlockSpec((1, D_MODEL), lambda i: (0, 0)),
            pl.BlockSpec((D_MODEL, qkv_dim), lambda i: (0, 0)),
        ],
        out_specs=out_specs,
        out_shape=out_shape,
        compiler_params=pltpu.CompilerParams(dimension_semantics=("arbitrary",),
                                             vmem_limit_bytes=VMEM_LIMIT),
        name="rms_qkv",
    )(x, g, w)


def _top3_select(gate, n_eligible):
    col = lax.broadcasted_iota(jnp.int32, gate.shape, 1).astype(_F32)
    neg = jnp.float32(-jnp.inf)
    gm = jnp.where(col < n_eligible, gate, neg)
    sel = jnp.zeros(gate.shape, _F32)
    for _ in range(MOBA_TOPK):
        mx = jnp.max(gm, axis=1, keepdims=True)
        first = jnp.min(jnp.where(gm == mx, col, jnp.float32(1e9)), axis=1, keepdims=True)
        pick = jnp.logical_and(col == first, mx > neg)
        sel = jnp.where(pick, 1.0, sel)
        gm = jnp.where(pick, neg, gm)
    return sel


def _stick_tri():
    j = lax.broadcasted_iota(jnp.int32, (STICK_KB, 2 * STICK_KB), 0)
    s = lax.broadcasted_iota(jnp.int32, (STICK_KB, 2 * STICK_KB), 1)
    return jnp.where(jnp.logical_or(s >= STICK_KB, j > s), 1.0, 0.0).astype(_BF16)


def _stick_chunk(z, causal, carry, tri, vc):
    sp = jnp.maximum(z, 0.0) + jnp.log1p(jnp.exp(-jnp.abs(z)))
    log_keep = -sp if causal is None else jnp.where(causal, -sp, 0.0)
    hi = log_keep.astype(_BF16)
    lo = (log_keep - hi.astype(_F32)).astype(_BF16)
    ct = (jnp.dot(hi, tri, preferred_element_type=_F32)
          + jnp.dot(lo, tri, preferred_element_type=_F32))
    log_after = carry + ct[:, :STICK_KB]
    a = jnp.exp((z - sp) + log_after)
    if causal is not None:
        a = jnp.where(causal, a, 0.0)
    pv = jnp.dot(a.astype(_BF16), vc, preferred_element_type=_F32)
    return pv, carry + ct[:, STICK_KB:]


def _moba_prompt_body(slopes_ref, q_ref, k_ref, v_ref, km_ref, o_ref, acc_ref, m_ref, l_ref,
                      *, tq):
    i = pl.program_id(1)
    o_ref[...] = jnp.zeros(o_ref.shape, o_ref.dtype)
    grp = _lane_group((tq, KV_DIM))
    row = lax.broadcasted_iota(jnp.int32, (tq, MOBA_BLOCK), 0)
    col = lax.broadcasted_iota(jnp.int32, (tq, MOBA_BLOCK), 1)
    rel = (row - col).astype(_F32)
    gcol = lax.broadcasted_iota(jnp.int32, (tq, GATE_LANES), 1)

    def head(hb, _):
        g = hb // N_KV_HEADS
        n = hb % N_KV_HEADS
        slope = slopes_ref[hb]
        qm = jnp.where(grp == n, q_ref[g] * SCALE, 0.0)
        qb = qm.astype(_BF16)
        gate = lax.dot_general(qm, km_ref[...], (((1,), (1,)), ((), ())),
                               precision=lax.Precision.HIGHEST, preferred_element_type=_F32)
        sel = _top3_select(gate, i.astype(_F32))

        off = pl.multiple_of(i * MOBA_BLOCK, MOBA_BLOCK)
        s = _nt_dot(qb, k_ref[pl.ds(off, MOBA_BLOCK), :]) - slope * rel
        s = jnp.where(rel >= 0.0, s, NEG_INF)
        m0 = jnp.max(s, axis=1, keepdims=True)
        p = jnp.exp(s - m0)
        m_ref[...] = m0
        l_ref[...] = jnp.sum(p, axis=1, keepdims=True)
        acc_ref[...] = jnp.dot(p.astype(_BF16), v_ref[pl.ds(off, MOBA_BLOCK), :],
                               preferred_element_type=_F32)

        def past(c, _):
            offc = pl.multiple_of(c * MOBA_BLOCK, MOBA_BLOCK)
            selc = jnp.sum(jnp.where(gcol == c, sel, 0.0), axis=1, keepdims=True) > 0.0
            dist = rel + ((i - c) * MOBA_BLOCK).astype(_F32)
            s = _nt_dot(qb, k_ref[pl.ds(offc, MOBA_BLOCK), :]) - slope * dist
            s = jnp.where(selc, s, NEG_INF)
            m_old = m_ref[...]
            m_new = jnp.maximum(m_old, jnp.max(s, axis=1, keepdims=True))
            alpha = jnp.exp(m_old - m_new)
            p = jnp.exp(s - m_new)
            m_ref[...] = m_new
            l_ref[...] = alpha * l_ref[...] + jnp.sum(p, axis=1, keepdims=True)
            acc_ref[...] = alpha * acc_ref[...] + jnp.dot(
                p.astype(_BF16), v_ref[pl.ds(offc, MOBA_BLOCK), :], preferred_element_type=_F32)
            return 0

        lax.fori_loop(0, i, past, 0)
        out = (acc_ref[...] / l_ref[...]).astype(o_ref.dtype)
        o_ref[g] = jnp.where(grp == n, out, o_ref[g])
        return 0

    lax.fori_loop(0, N_HEADS, head, 0)


def _moba_prompt(slopes, q, kb, vb, km, *, batch, seq):
    tq = MOBA_BLOCK
    nq = seq // tq
    return pl.pallas_call(
        functools.partial(_moba_prompt_body, tq=tq),
        grid=(batch, nq),
        in_specs=[
            pl.BlockSpec(memory_space=pltpu.SMEM),
            pl.BlockSpec((GROUP, tq, KV_DIM), lambda b, i: (0, b * nq + i, 0)),
            pl.BlockSpec((seq, KV_DIM), lambda b, i: (b, 0)),
            pl.BlockSpec((seq, KV_DIM), lambda b, i: (b, 0)),
            pl.BlockSpec((GATE_LANES, KV_DIM), lambda b, i: (b, 0)),
        ],
        out_specs=pl.BlockSpec((GROUP, tq, KV_DIM), lambda b, i: (0, b * nq + i, 0)),
        out_shape=jax.ShapeDtypeStruct((GROUP, batch * seq, KV_DIM), _BF16),
        scratch_shapes=[pltpu.VMEM((tq, KV_DIM), _F32), pltpu.VMEM((tq, 1), _F32),
                        pltpu.VMEM((tq, 1), _F32)],
        compiler_params=pltpu.CompilerParams(dimension_semantics=("arbitrary", "arbitrary"),
                                             vmem_limit_bytes=VMEM_LIMIT),
        name="moba_prompt",
    )(slopes, q, kb, vb, km)


def _stick_prompt_body(q_ref, k_ref, v_ref, o_ref, acc_ref, carry_ref, *, tq):
    i = pl.program_id(1)
    o_ref[...] = jnp.zeros(o_ref.shape, o_ref.dtype)
    grp = _lane_group((tq, KV_DIM))
    tri = _stick_tri()
    row = lax.broadcasted_iota(jnp.int32, (tq, STICK_KB), 0)
    col = lax.broadcasted_iota(jnp.int32, (tq, STICK_KB), 1)
    rel = row - col
    last_chunk = (i + 1) * (tq // STICK_KB) - 1

    def head(hb, _):
        g = hb // N_KV_HEADS
        n = hb % N_KV_HEADS
        qb = jnp.where(grp == n, q_ref[g] * SCALE, 0.0).astype(_BF16)
        acc_ref[...] = jnp.zeros(acc_ref.shape, _F32)
        carry_ref[...] = jnp.zeros(carry_ref.shape, _F32)

        def cond(state):
            c, alive = state
            return jnp.logical_and(c >= 0, alive > 0)

        def step(state):
            c, _ = state
            off = pl.multiple_of(c * STICK_KB, STICK_KB)
            z = _nt_dot(qb, k_ref[pl.ds(off, STICK_KB), :])
            causal = rel + (i * tq - c * STICK_KB) > 0
            pv, carry = _stick_chunk(z, causal, carry_ref[...], tri, v_ref[pl.ds(off, STICK_KB), :])
            acc_ref[...] += pv
            carry_ref[...] = carry
            alive = (jnp.max(carry) > STICK_DEAD).astype(jnp.int32)
            return c - 1, alive

        lax.while_loop(cond, step, (last_chunk, jnp.int32(1)))
        o_ref[g] = jnp.where(grp == n, acc_ref[...].astype(o_ref.dtype), o_ref[g])
        return 0

    lax.fori_loop(0, N_HEADS, head, 0)


def _stick_prompt(q, kb, vb, *, batch, seq):
    tq = 256
    nq = seq // tq
    return pl.pallas_call(
        functools.partial(_stick_prompt_body, tq=tq),
        grid=(batch, nq),
        in_specs=[
            pl.BlockSpec((GROUP, tq, KV_DIM), lambda b, i: (0, b * nq + i, 0)),
            pl.BlockSpec((seq, KV_DIM), lambda b, i: (b, 0)),
            pl.BlockSpec((seq, KV_DIM), lambda b, i: (b, 0)),
        ],
        out_specs=pl.BlockSpec((GROUP, tq, KV_DIM), lambda b, i: (0, b * nq + i, 0)),
        out_shape=jax.ShapeDtypeStruct((GROUP, batch * seq, KV_DIM), _BF16),
        scratch_shapes=[pltpu.VMEM((tq, KV_DIM), _F32), pltpu.VMEM((tq, STICK_KB), _F32)],
        compiler_params=pltpu.CompilerParams(dimension_semantics=("arbitrary", "arbitrary"),
                                             vmem_limit_bytes=VMEM_LIMIT),
        name="stick_prompt",
    )(q, kb, vb)


def _sample_rows(q_ref):
    q = q_ref[0] * SCALE
    grp = _lane_group(q.shape)
    return jnp.concatenate([jnp.where(grp == n, q, 0.0) for n in range(N_KV_HEADS)], axis=0)


def _sample_out(acc):
    r = acc.shape[0] // N_KV_HEADS
    grp = _lane_group((r, KV_DIM))
    out = jnp.zeros((r, KV_DIM), _F32)
    for n in range(N_KV_HEADS):
        out = jnp.where(grp == n, acc[n * r:(n + 1) * r], out)
    return out


def _moba_sample_body(pt_ref, slope_ref, qpos_ref, q_ref, kown_ref, vown_ref, *rest, n_steps):
    kp = rest[:PAGES_PER_STEP]
    vp = rest[PAGES_PER_STEP:2 * PAGES_PER_STEP]
    o_ref, oall_ref, mtab_ref, ltab_ref, gtab_ref = rest[2 * PAGES_PER_STEP:]
    sp = pl.program_id(1)
    qm = _sample_rows(q_ref)
    qb = qm.astype(_BF16)
    rows = qm.shape[0]
    slope = slope_ref[...]
    qpos = qpos_ref[...]
    gcol = lax.broadcasted_iota(jnp.int32, (rows, GATE_LANES), 1)
    kcol = lax.broadcasted_iota(jnp.int32, (rows, MOBA_BLOCK), 1).astype(_F32)
    pages_per_block = MOBA_BLOCK // PAGE_SIZE
    blocks_per_step = PAGES_PER_STEP // pages_per_block

    @pl.when(sp == 0)
    def _():
        for tab in (mtab_ref, ltab_ref, gtab_ref):
            tab[...] = jnp.zeros(tab.shape, _F32)

    for jb in range(blocks_per_step):
        m = sp * blocks_per_step + jb
        kblk = jnp.concatenate([kp[jb * pages_per_block + u][0] for u in range(pages_per_block)], axis=0)
        vblk = jnp.concatenate([vp[jb * pages_per_block + u][0] for u in range(pages_per_block)], axis=0)
        kmean = jnp.mean(kblk, axis=0, keepdims=True)
        gate = jnp.sum(qm * kmean, axis=1, keepdims=True)
        kpos = kcol + (m * MOBA_BLOCK).astype(_F32)
        s = _nt_dot(qb, kblk.astype(_BF16)) - slope * (qpos - kpos)
        mb = jnp.max(s, axis=1, keepdims=True)
        p = jnp.exp(s - mb)
        oall_ref[m] = jnp.dot(p.astype(_BF16), vblk.astype(_BF16), preferred_element_type=_F32)
        here = gcol == m
        mtab_ref[...] = jnp.where(here, mb, mtab_ref[...])
        ltab_ref[...] = jnp.where(here, jnp.sum(p, axis=1, keepdims=True), ltab_ref[...])
        gtab_ref[...] = jnp.where(here, gate, gtab_ref[...])

    @pl.when(sp == n_steps - 1)
    def _():
        n_blocks = n_steps * blocks_per_step
        ocol = lax.broadcasted_iota(jnp.int32, (rows, PAGE_SIZE), 1).astype(_F32)
        dist = (qpos - (n_blocks * MOBA_BLOCK)) - ocol
        s = _nt_dot(qb, kown_ref[0]) - slope * dist
        s = jnp.where(dist >= 0.0, s, NEG_INF)
        m_own = jnp.max(s, axis=1, keepdims=True)
        p = jnp.exp(s - m_own)
        l_own = jnp.sum(p, axis=1, keepdims=True)
        o_own = jnp.dot(p.astype(_BF16), vown_ref[0], preferred_element_type=_F32)

        sel = _top3_select(gtab_ref[...], jnp.float32(n_blocks))
        picked = sel > 0.0
        mtab = mtab_ref[...]
        m_all = jnp.maximum(m_own, jnp.max(jnp.where(picked, mtab, NEG_INF), axis=1, keepdims=True))
        w = jnp.where(picked, jnp.exp(mtab - m_all), 0.0)
        w_own = jnp.exp(m_own - m_all)
        l_all = w_own * l_own + jnp.sum(w * ltab_ref[...], axis=1, keepdims=True)

        def merge(mi, acc):
            wm = jnp.sum(jnp.where(gcol == mi, w, 0.0), axis=1, keepdims=True)
            return acc + wm * oall_ref[mi]

        acc = lax.fori_loop(0, n_blocks, merge, w_own * o_own)
        o_ref[0] = _sample_out(acc / l_all).astype(o_ref.dtype)


def _moba_sample(page_table, slope_rows, qpos_rows, q_rows, k_own, v_own, cache_k, cache_v):
    dec_b, n_pages = page_table.shape
    n_steps = n_pages // PAGES_PER_STEP
    rows = q_rows.shape[1] * N_KV_HEADS

    def page_spec(j):
        return pl.BlockSpec((1, PAGE_SIZE, KV_DIM),
                            lambda b, s, pt: (pt[b, s * PAGES_PER_STEP + j], 0, 0))

    n_blocks = n_pages * PAGE_SIZE // MOBA_BLOCK
    grid_spec = pltpu.PrefetchScalarGridSpec(
        num_scalar_prefetch=1,
        grid=(dec_b, n_steps),
        in_specs=[
            pl.BlockSpec((rows, 1), lambda b, s, pt: (0, 0)),
            pl.BlockSpec((rows, 1), lambda b, s, pt: (0, 0)),
            pl.BlockSpec((1, q_rows.shape[1], KV_DIM), lambda b, s, pt: (b, 0, 0)),
            pl.BlockSpec((1, PAGE_SIZE, KV_DIM), lambda b, s, pt: (b, 0, 0)),
            pl.BlockSpec((1, PAGE_SIZE, KV_DIM), lambda b, s, pt: (b, 0, 0)),
        ] + [page_spec(j) for j in range(PAGES_PER_STEP)] * 2,
        out_specs=pl.BlockSpec((1, q_rows.shape[1], KV_DIM), lambda b, s, pt: (b, 0, 0)),
        scratch_shapes=[pltpu.VMEM((n_blocks, rows, KV_DIM), _F32),
                        pltpu.VMEM((rows, GATE_LANES), _F32),
                        pltpu.VMEM((rows, GATE_LANES), _F32),
                        pltpu.VMEM((rows, GATE_LANES), _F32)],
    )
    return pl.pallas_call(
        functools.partial(_moba_sample_body, n_steps=n_steps),
        grid_spec=grid_spec,
        out_shape=jax.ShapeDtypeStruct(q_rows.shape, _BF16),
        compiler_params=pltpu.CompilerParams(dimension_semantics=("arbitrary", "arbitrary"),
                                             vmem_limit_bytes=VMEM_LIMIT),
        name="moba_sample",
    )(page_table, slope_rows, qpos_rows, q_rows, k_own, v_own,
      *([cache_k] * PAGES_PER_STEP), *([cache_v] * PAGES_PER_STEP))


def _stick_sample_body(pt_ref, tok_ref, q_ref, kown_ref, vown_ref, *rest, n_steps):
    kp = rest[:PAGES_PER_STEP]
    vp = rest[PAGES_PER_STEP:2 * PAGES_PER_STEP]
    o_ref, acc_ref, carry_ref, alive_ref = rest[2 * PAGES_PER_STEP:]
    sp = pl.program_id(1)
    qb = _sample_rows(q_ref).astype(_BF16)
    rows = qb.shape[0]
    tri = _stick_tri()

    @pl.when(sp == 0)
    def _():
        kcol = lax.broadcasted_iota(jnp.int32, (rows, STICK_KB), 1).astype(_F32)
        causal = kcol < tok_ref[...]
        z = _nt_dot(qb, kown_ref[0])
        pv, carry = _stick_chunk(z, causal, jnp.zeros((rows, STICK_KB), _F32), tri, vown_ref[0])
        acc_ref[...] = pv
        carry_ref[...] = carry
        alive_ref[0] = (jnp.max(carry) > STICK_DEAD).astype(jnp.int32)

    for j in range(PAGES_PER_STEP):
        @pl.when(alive_ref[0] > 0)
        def _():
            z = _nt_dot(qb, kp[j][0].astype(_BF16))
            pv, carry = _stick_chunk(z, None, carry_ref[...], tri, vp[j][0].astype(_BF16))
            acc_ref[...] += pv
            carry_ref[...] = carry
            alive_ref[0] = (jnp.max(carry) > STICK_DEAD).astype(jnp.int32)

    @pl.when(sp == n_steps - 1)
    def _():
        o_ref[0] = _sample_out(acc_ref[...]).astype(o_ref.dtype)


def _stick_sample(page_table, tok_rows, q_rows, k_own, v_own, cache_k, cache_v):
    dec_b, n_pages = page_table.shape
    n_steps = n_pages // PAGES_PER_STEP
    rows = q_rows.shape[1] * N_KV_HEADS

    def page_spec(j):
        return pl.BlockSpec((1, PAGE_SIZE, KV_DIM),
                            lambda b, s, pt: (pt[b, n_pages - 1 - (s * PAGES_PER_STEP + j)], 0, 0))

    grid_spec = pltpu.PrefetchScalarGridSpec(
        num_scalar_prefetch=1,
        grid=(dec_b, n_steps),
        in_specs=[
            pl.BlockSpec((rows, 1), lambda b, s, pt: (0, 0)),
            pl.BlockSpec((1, q_rows.shape[1], KV_DIM), lambda b, s, pt: (b, 0, 0)),
            pl.BlockSpec((1, PAGE_SIZE, KV_DIM), lambda b, s, pt: (b, 0, 0)),
            pl.BlockSpec((1, PAGE_SIZE, KV_DIM), lambda b, s, pt: (b, 0, 0)),
        ] + [page_spec(j) for j in range(PAGES_PER_STEP)] * 2,
        out_specs=pl.BlockSpec((1, q_rows.shape[1], KV_DIM), lambda b, s, pt: (b, 0, 0)),
        scratch_shapes=[pltpu.VMEM((rows, KV_DIM), _F32),
                        pltpu.VMEM((rows, STICK_KB), _F32),
                        pltpu.SMEM((1,), jnp.int32)],
    )
    return pl.pallas_call(
        functools.partial(_stick_sample_body, n_steps=n_steps),
        grid_spec=grid_spec,
        out_shape=jax.ShapeDtypeStruct(q_rows.shape, _BF16),
        compiler_params=pltpu.CompilerParams(dimension_semantics=("arbitrary", "arbitrary"),
                                             vmem_limit_bytes=VMEM_LIMIT),
        name="stick_sample",
    )(page_table, tok_rows, q_rows, k_own, v_own,
      *([cache_k] * PAGES_PER_STEP), *([cache_v] * PAGES_PER_STEP))


def _ffn_body(*refs, tm, tiles_per_seq, sample, final):
    it = iter(refs)
    x_ref, o_ref, wo_ref, g_ref, wup_ref, cw_ref, cb_ref, wdn_ref = (next(it) for _ in range(8))
    s1_ref = s2_ref = tok_ref = gfin_ref = None
    if sample:
        s1_ref, s2_ref, tok_ref = next(it), next(it), next(it)
    if final:
        gfin_ref = next(it)
    xo_ref, u_ref = next(it), next(it)
    y_ref = next(it) if final else None
    ubuf, carry, acc_ref = next(it), next(it), next(it)

    i = pl.program_id(0)
    x1 = x_ref[...]
    for g in range(GROUP):
        x1 = x1 + jnp.dot(o_ref[g], wo_ref[g], preferred_element_type=_F32)
    ms = jnp.mean(x1 * x1, axis=-1, keepdims=True)
    h = ((x1 * lax.rsqrt(ms + RMS_EPS)) * g_ref[...]).astype(_BF16)

    if not sample:
        @pl.when(i % tiles_per_seq == 0)
        def _():
            carry[...] = jnp.zeros(carry.shape, _F32)
    acc_ref[...] = jnp.zeros(acc_ref.shape, _F32)
    ubuf[0:8, :] = jnp.zeros((8, FF_CHUNK), _F32)

    def conv(jj):
        u = jnp.dot(h, wup_ref[jj], preferred_element_type=_F32)
        ubuf[8:8 + tm, :] = u
        if sample:
            u_ref[jj] = u
            tok = tok_ref[...]
            u1 = jnp.where(tok < 1.0, s1_ref[jj], ubuf[7:7 + tm, :])
            u2 = jnp.where(tok < 2.0, s2_ref[jj], ubuf[6:6 + tm, :])
        else:
            ubuf[6:8, :] = carry[jj, 6:8, :]
            u1 = ubuf[7:7 + tm, :]
            u2 = ubuf[6:6 + tm, :]
            carry[jj, 6:8, :] = u[tm - 2:tm, :]
        w = cw_ref[jj]
        return cb_ref[jj] + w[0:1, :] * u2 + w[1:2, :] * u1 + w[2:3, :] * u

    def chunk(j, _):
        cg = conv(j)
        cv = conv(j + N_FF_CHUNKS)
        act = (cg / (1.0 + jnp.exp(-cg))) * cv
        acc_ref[...] += jnp.dot(act.astype(_BF16), wdn_ref[j], preferred_element_type=_F32)
        return 0

    lax.fori_loop(0, N_FF_CHUNKS, chunk, 0)
    xo = x1 + acc_ref[...]
    xo_ref[...] = xo
    if not sample:
        u_ref[...] = carry[...]
    if final:
        ms2 = jnp.mean(xo * xo, axis=-1, keepdims=True)
        y_ref[...] = (xo * lax.rsqrt(ms2 + RMS_EPS)) * gfin_ref[...]


def _oproj_ffn(x, o, wo, g, wup, cw, cb, wdn, *, tm, seq, sample_state=None, g_final=None):
    t = x.shape[0]
    sample = sample_state is not None
    final = g_final is not None
    n_tiles = t // tm
    tiles_per_seq = max(seq // tm, 1)
    const2 = lambda i: (0, 0)
    const3 = lambda i: (0, 0, 0)
    in_specs = [
        pl.BlockSpec((tm, D_MODEL), lambda i: (i, 0)),
        pl.BlockSpec((GROUP, tm, KV_DIM), lambda i: (0, i, 0)),
        pl.BlockSpec(wo.shape, const3),
        pl.BlockSpec((1, D_MODEL), const2),
        pl.BlockSpec(wup.shape, const3),
        pl.BlockSpec(cw.shape, const3),
        pl.BlockSpec(cb.shape, const3),
        pl.BlockSpec(wdn.shape, const3),
    ]
    args = [x, o, wo, g, wup, cw, cb, wdn]
    if sample:
        s1, s2, tok = sample_state
        in_specs += [pl.BlockSpec(s1.shape, const3), pl.BlockSpec(s2.shape, const3),
                     pl.BlockSpec(tok.shape, const2)]
        args += [s1, s2, tok]
    if final:
        in_specs.append(pl.BlockSpec((1, D_MODEL), const2))
        args.append(g_final)
    out_shape = [jax.ShapeDtypeStruct((t, D_MODEL), _F32)]
    out_specs = [pl.BlockSpec((tm, D_MODEL), lambda i: (i, 0))]
    if sample:
        out_shape.append(jax.ShapeDtypeStruct((2 * N_FF_CHUNKS, t, FF_CHUNK), _F32))
        out_specs.append(pl.BlockSpec((2 * N_FF_CHUNKS, tm, FF_CHUNK), lambda i: (0, i, 0)))
    else:
        n_seq = t // seq
        out_shape.append(jax.ShapeDtypeStruct((n_seq, 2 * N_FF_CHUNKS, 8, FF_CHUNK), _F32))
        out_specs.append(pl.BlockSpec((None, 2 * N_FF_CHUNKS, 8, FF_CHUNK),
                                      lambda i: (i // tiles_per_seq, 0, 0, 0)))
    if final:
        out_shape.append(jax.ShapeDtypeStruct((t, D_MODEL), _F32))
        out_specs.append(pl.BlockSpec((tm, D_MODEL), lambda i: (i, 0)))
    return pl.pallas_call(
        functools.partial(_ffn_body, tm=tm, tiles_per_seq=tiles_per_seq, sample=sample, final=final),
        grid=(n_tiles,),
        in_specs=in_specs,
        out_specs=out_specs,
        out_shape=out_shape,
        scratch_shapes=[pltpu.VMEM((tm + 8, FF_CHUNK), _F32),
                        pltpu.VMEM((2 * N_FF_CHUNKS, 8, FF_CHUNK), _F32),
                        pltpu.VMEM((tm, D_MODEL), _F32)],
        compiler_params=pltpu.CompilerParams(dimension_semantics=("arbitrary",),
                                             vmem_limit_bytes=VMEM_LIMIT),
        name="oproj_ffn_sample" if sample else "oproj_ffn",
    )(*args)


def _chunk_cols(a):
    lead = a.shape[:-2]
    r = a.shape[-2]
    a = a.reshape(lead + (r, 2 * N_FF_CHUNKS, FF_CHUNK))
    return jnp.moveaxis(a, -2, -3)


def _unchunk_cols(a):
    a = jnp.moveaxis(a, -3, -2)
    return a.reshape(a.shape[:-2] + (2 * D_FF,))


def kernel(x_prompt, x_sample, cache_k, cache_v, state_conv, page_table, g_attn, w_qkv, w_o,
           g_ffn, w_up, conv_w, conv_b, w_down, g_final):
    depth = w_qkv.shape[0]
    batch, seq, _ = x_prompt.shape
    dec_b, dec_s, _ = x_sample.shape
    n_pool = cache_k.shape[1]
    n_pages = page_table.shape[1]
    past = n_pages * PAGE_SIZE
    tp = batch * seq
    ts = dec_b * dec_s

    wq = w_qkv[:, :, :D_MODEL].reshape(depth, D_MODEL, N_KV_HEADS, GROUP, HEAD_DIM)
    wq = wq.transpose(0, 1, 3, 2, 4).reshape(depth, D_MODEL, D_MODEL)
    wqkv = jnp.concatenate([wq, w_qkv[:, :, D_MODEL:]], axis=2).astype(_BF16)
    wo = w_o.reshape(depth, N_KV_HEADS, GROUP, HEAD_DIM, D_MODEL).transpose(0, 2, 1, 3, 4)
    wo = wo.reshape(depth, GROUP, KV_DIM, D_MODEL).astype(_BF16)
    wup = _chunk_cols(w_up).astype(_BF16)
    cw = _chunk_cols(conv_w)
    cb = _chunk_cols(conv_b[:, None, :])
    wdn = w_down.reshape(depth, N_FF_CHUNKS, FF_CHUNK, D_MODEL).astype(_BF16)
    g_attn2 = g_attn[:, None, :]
    g_ffn2 = g_ffn[:, None, :]
    g_fin2 = g_final[None, :]

    hidx = jnp.arange(1, N_HEADS + 1, dtype=_F32)
    slopes = jnp.exp2(-8.0 * hidx / N_HEADS).reshape(N_KV_HEADS, GROUP)
    slopes_gn = slopes.T.reshape(N_HEADS)
    slope_rows = jnp.repeat(slopes.reshape(N_HEADS), dec_s)[:, None]
    tok_rows = jnp.tile(jnp.arange(dec_s, dtype=_F32), N_HEADS)[:, None]
    qpos_rows = tok_rows + float(past)
    tok_seq = jnp.tile(jnp.arange(dec_s, dtype=_F32), dec_b)[:, None]

    cache_k3 = cache_k.reshape(depth, n_pool, PAGE_SIZE, KV_DIM)
    cache_v3 = cache_v.reshape(depth, n_pool, PAGE_SIZE, KV_DIM)

    st = state_conv
    zero = jnp.zeros_like(st[:, :, :1])
    s1 = jnp.concatenate([st[:, :, 1:2], zero, zero, zero][:dec_s], axis=2)
    s2 = jnp.concatenate([st[:, :, 0:1], st[:, :, 1:2], zero, zero][:dec_s], axis=2)
    s1 = _chunk_cols(s1.reshape(depth, ts, 2 * D_FF))
    s2 = _chunk_cols(s2.reshape(depth, ts, 2 * D_FF))

    xp = x_prompt.reshape(tp, D_MODEL)
    xs = x_sample.reshape(ts, D_MODEL)
    kp_l, vp_l, ks_l, vs_l, cp_l, cs_l = [], [], [], [], [], []
    yp = ys = None
    pad_own = jnp.zeros((dec_b, PAGE_SIZE - dec_s, KV_DIM), _BF16)
    for l in range(depth):
        last = l == depth - 1
        gfin = g_fin2 if last else None
        moba = l % 2 == 0
        q, k, v, kb, vb, *km = _rms_qkv(xp, g_attn2[l], wqkv[l], tm=512, with_kmean=moba)
        if moba:
            nb = seq // MOBA_BLOCK
            kmp = km[0].reshape(batch, nb, KV_DIM)
            kmp = jnp.pad(kmp, ((0, 0), (0, GATE_LANES - nb), (0, 0))).reshape(batch * GATE_LANES, KV_DIM)
            o = _moba_prompt(slopes_gn, q, kb, vb, kmp, batch=batch, seq=seq)
        else:
            o = _stick_prompt(q, kb, vb, batch=batch, seq=seq)
        outs = _oproj_ffn(xp, o, wo[l], g_ffn2[l], wup[l], cw[l], cb[l], wdn[l],
                          tm=512, seq=seq, g_final=gfin)
        xp, cst = outs[0], outs[1]
        if last:
            yp = outs[2]
        kp_l.append(k.reshape(batch, seq, N_KV_HEADS, HEAD_DIM))
        vp_l.append(v.reshape(batch, seq, N_KV_HEADS, HEAD_DIM))
        cp_l.append(_unchunk_cols(cst[:, :, 6:8, :]))
        qs, ksn, vsn, ksb, vsb = _rms_qkv(xs, g_attn2[l], wqkv[l], tm=ts, with_kmean=False)
        q_rows = qs.reshape(GROUP, dec_b, dec_s, KV_DIM).transpose(1, 0, 2, 3).reshape(dec_b, GROUP * dec_s, KV_DIM)
        k_own = jnp.concatenate([ksb.reshape(dec_b, dec_s, KV_DIM), pad_own], axis=1)
        v_own = jnp.concatenate([vsb.reshape(dec_b, dec_s, KV_DIM), pad_own], axis=1)
        if moba:
            o_rows = _moba_sample(page_table, slope_rows, qpos_rows, q_rows, k_own, v_own,
                                  cache_k3[l], cache_v3[l])
        else:
            o_rows = _stick_sample(page_table, tok_rows, q_rows, k_own, v_own, cache_k3[l], cache_v3[l])
        os_ = o_rows.reshape(dec_b, GROUP, dec_s, KV_DIM).transpose(1, 0, 2, 3).reshape(GROUP, ts, KV_DIM)
        outs = _oproj_ffn(xs, os_, wo[l], g_ffn2[l], wup[l], cw[l], cb[l], wdn[l],
                          tm=ts, seq=dec_s, sample_state=(s1[l], s2[l], tok_seq), g_final=gfin)
        xs, u_all = outs[0], outs[1]
        if last:
            ys = outs[2]
        ks_l.append(ksn.reshape(dec_b, dec_s, N_KV_HEADS, HEAD_DIM))
        vs_l.append(vsn.reshape(dec_b, dec_s, N_KV_HEADS, HEAD_DIM))
        u_full = _unchunk_cols(u_all).reshape(dec_b, dec_s, 2 * D_FF)
        cs_l.append(u_full[:, dec_s - (CONV_W - 1):, :])
    return (yp.reshape(batch, seq, D_MODEL), ys.reshape(dec_b, dec_s, D_MODEL),
            jnp.stack(kp_l), jnp.stack(vp_l), jnp.stack(ks_l), jnp.stack(vs_l),
            jnp.stack(cp_l), jnp.stack(cs_l))
```

```python
import functools

import jax
import jax.numpy as jnp
from jax import lax
from jax.experimental import pallas as pl
from jax.experimental.pallas import tpu as pltpu

D_MODEL = 1024
N_HEADS = 16
HEAD_DIM = 64
N_KV_HEADS = 4
GROUP = N_HEADS // N_KV_HEADS
KV_DIM = N_KV_HEADS * HEAD_DIM
SLAB = 2 * HEAD_DIM
MOBA_BLOCK = 256
MOBA_TOPK = 3
D_FF = 2816
CONV_W = 3
PAGE_SIZE = 128
RMS_EPS = 1e-6
NEG_INF = -1e30
SCALE = HEAD_DIM ** -0.5

FF_CHUNK = 256
N_FF_CHUNKS = D_FF // FF_CHUNK
STICK_KB = 128
STICK_DEAD = -110.0
GATE_LANES = 128
FLAG_LANE0 = HEAD_DIM
SLOPE_LANE0 = 96
PAGES_PER_STEP = 8
VMEM_LIMIT = 56 * 1024 * 1024
TQ = 256

_BF16 = jnp.bfloat16
_F32 = jnp.float32


def _nt_dot(a, b):
    return lax.dot_general(a, b, (((1,), (1,)), ((), ())), preferred_element_type=_F32)


def _lane_group(shape):
    return lax.broadcasted_iota(jnp.int32, shape, len(shape) - 1) >> 6


def _qkv_body(x_ref, g_ref, w_ref, q_ref, k_ref, v_ref, ks_ref, vs_ref, *km_ref, tm):
    x = x_ref[...]
    ms = jnp.mean(x * x, axis=-1, keepdims=True)
    h = (x * lax.rsqrt(ms + RMS_EPS)) * g_ref[...]
    qkv = jnp.dot(h.astype(_BF16), w_ref[...], preferred_element_type=_F32)
    for j in range(N_HEADS // 2):
        q_ref[2 * j] = qkv[:, j * SLAB:(j + 1) * SLAB]
        q_ref[2 * j + 1] = qkv[:, D_MODEL + j * SLAB:D_MODEL + (j + 1) * SLAB]
    base = 2 * D_MODEL
    k = qkv[:, base:base + KV_DIM]
    v = qkv[:, base + KV_DIM:base + 2 * KV_DIM]
    k_sw = qkv[:, base + 2 * KV_DIM:base + 3 * KV_DIM]
    v_sw = qkv[:, base + 3 * KV_DIM:base + 4 * KV_DIM]
    k_ref[...] = k
    v_ref[...] = v
    for j in range(N_KV_HEADS // 2):
        ks_ref[2 * j] = k[:, j * SLAB:(j + 1) * SLAB].astype(_BF16)
        ks_ref[2 * j + 1] = k_sw[:, j * SLAB:(j + 1) * SLAB].astype(_BF16)
        vs_ref[2 * j] = v[:, j * SLAB:(j + 1) * SLAB].astype(_BF16)
        vs_ref[2 * j + 1] = v_sw[:, j * SLAB:(j + 1) * SLAB].astype(_BF16)
    if km_ref:
        for j in range(tm // MOBA_BLOCK):
            km_ref[0][j] = jnp.mean(k[j * MOBA_BLOCK:(j + 1) * MOBA_BLOCK], axis=0, keepdims=True)


def _rms_qkv(x, g, w, *, tm, with_kmean):
    t = x.shape[0]
    out_shape = [
        jax.ShapeDtypeStruct((N_HEADS, t, SLAB), _F32),
        jax.ShapeDtypeStruct((t, KV_DIM), _F32),
        jax.ShapeDtypeStruct((t, KV_DIM), _F32),
        jax.ShapeDtypeStruct((N_KV_HEADS, t, SLAB), _BF16),
        jax.ShapeDtypeStruct((N_KV_HEADS, t, SLAB), _BF16),
    ]
    row = pl.BlockSpec((tm, KV_DIM), lambda i: (i, 0))
    slab = pl.BlockSpec((N_KV_HEADS, tm, SLAB), lambda i: (0, i, 0))
    out_specs = [pl.BlockSpec((N_HEADS, tm, SLAB), lambda i: (0, i, 0)), row, row, slab, slab]
    if with_kmean:
        out_shape.append(jax.ShapeDtypeStruct((t // MOBA_BLOCK, 1, KV_DIM), _F32))
        out_specs.append(pl.BlockSpec((tm // MOBA_BLOCK, 1, KV_DIM), lambda i: (i, 0, 0)))
    return pl.pallas_call(
        functools.partial(_qkv_body, tm=tm),
        grid=(t // tm,),
        in_specs=[
            pl.BlockSpec((tm, D_MODEL), lambda i: (i, 0)),
            pl.BlockSpec((1, D_MODEL), lambda i: (0, 0)),
            pl.BlockSpec(w.shape, lambda i: (0, 0)),
        ],
        out_specs=out_specs,
        out_shape=out_shape,
        compiler_params=pltpu.CompilerParams(dimension_semantics=("arbitrary",),
                                             vmem_limit_bytes=VMEM_LIMIT),
        name="rms_qkv",
    )(x, g, w)


def _top3_select(gate, eligible, axis):
    idx = lax.broadcasted_iota(jnp.int32, gate.shape, axis).astype(_F32)
    neg = jnp.float32(-jnp.inf)
    gm = jnp.where(eligible, gate, neg)
    sel = jnp.zeros(gate.shape, _F32)
    for _ in range(MOBA_TOPK):
        mx = jnp.max(gm, axis=axis, keepdims=True)
        first = jnp.min(jnp.where(gm == mx, idx, jnp.float32(1e9)), axis=axis, keepdims=True)
        pick = jnp.logical_and(idx == first, mx > neg)
        sel = jnp.where(pick, 1.0, sel)
        gm = jnp.where(pick, neg, gm)
    return sel


def _stick_tri():
    j = lax.broadcasted_iota(jnp.int32, (2 * STICK_KB, 2 * STICK_KB), 0) & (STICK_KB - 1)
    s = lax.broadcasted_iota(jnp.int32, (2 * STICK_KB, 2 * STICK_KB), 1)
    return jnp.where(jnp.logical_or(s >= STICK_KB, j > s), 1.0, 0.0).astype(_BF16)


def _stick_chunk(z, causal, carry, tri, vc_nt=None, vc=None):
    nz = -z
    e = jnp.exp(jnp.minimum(z, nz))
    log_keep = jnp.minimum(nz, 0.0) - jnp.log(1.0 + e)
    log_sig = z + log_keep
    if causal is not None:
        log_keep = jnp.where(causal, log_keep, 0.0)
    hi = log_keep.astype(_BF16)
    lo = (log_keep - hi.astype(_F32)).astype(_BF16)
    ct = jnp.dot(jnp.concatenate([hi, lo], axis=1), tri, preferred_element_type=_F32)
    log_after = carry + ct[:, :STICK_KB]
    a = jnp.exp(log_sig + log_after)
    if causal is not None:
        a = jnp.where(causal, a, 0.0)
    a = a.astype(_BF16)
    pv = _nt_dot(a, vc_nt) if vc is None else jnp.dot(a, vc, preferred_element_type=_F32)
    return pv, carry + ct[:, STICK_KB:]


def _pair_heads(acc_ref, o_ref, normalise):
    low = lax.broadcasted_iota(jnp.int32, acc_ref.shape[1:], 1) < HEAD_DIM
    for j in range(N_HEADS // 2):
        a0 = acc_ref[2 * j]
        a1 = acc_ref[2 * j + 1]
        if normalise:
            a0 = a0 / pltpu.roll(a0, HEAD_DIM, 1)
            a1 = a1 / pltpu.roll(a1, HEAD_DIM, 1)
        o_ref[:, j * SLAB:(j + 1) * SLAB] = jnp.where(low, a0, pltpu.roll(a1, HEAD_DIM, 1)).astype(o_ref.dtype)


def _moba_prompt_body(stab_ref, q_ref, k_ref, v_ref, km_ref, o_ref, qa_ref, m_ref, acc_ref, *, tq):
    i = pl.program_id(1)
    rows = GROUP * tq
    low = lax.broadcasted_iota(jnp.int32, (tq, SLAB), 1) < HEAD_DIM
    trow = lax.broadcasted_iota(jnp.int32, (GATE_LANES, tq), 0)
    eligible = jnp.logical_and(trow >= FLAG_LANE0, trow < FLAG_LANE0 + i)
    krow = lax.broadcasted_iota(jnp.int32, (MOBA_BLOCK, SLAB), 0)
    klane = lax.broadcasted_iota(jnp.int32, (MOBA_BLOCK, SLAB), 1)
    klow = klane < HEAD_DIM
    one = jnp.ones((MOBA_BLOCK, SLAB), _BF16)

    def key_aux(block_offset, c):
        f = jnp.where(klane == FLAG_LANE0 + c, NEG_INF, 0.0)
        f = jnp.where(jnp.logical_and(klane >= SLOPE_LANE0, klane < SLOPE_LANE0 + 3), block_offset, f)
        f = jnp.where(jnp.logical_and(klane >= SLOPE_LANE0 + 3, klane < SLOPE_LANE0 + 6),
                      krow.astype(_F32), f)
        return f.astype(_BF16)

    def kv_head(n, _):
        km = km_ref[n]
        for g in range(GROUP):
            h = n * GROUP + g
            qh = q_ref[h]
            gate_t = lax.dot_general(km, qh, (((1,), (1,)), ((), ())),
                                     precision=lax.Precision.HIGHEST, preferred_element_type=_F32)
            sel_t = _top3_select(gate_t, eligible, 0)
            notsel = jnp.where(eligible, 1.0 - sel_t, 0.0).T
            qa_ref[h] = jnp.where(low, qh * SCALE, notsel + stab_ref[h][0:1, :]).astype(_BF16)
        return 0

    lax.fori_loop(0, N_KV_HEADS, kv_head, 0)

    off = pl.multiple_of(i * MOBA_BLOCK, MOBA_BLOCK)
    f_own = key_aux(0.0, i)
    qrow = lax.broadcasted_iota(jnp.int32, (rows, MOBA_BLOCK), 0) & (tq - 1)
    kcol = lax.broadcasted_iota(jnp.int32, (rows, MOBA_BLOCK), 1)
    causal = qrow >= kcol
    for n in range(N_KV_HEADS):
        hs = pl.ds(n * GROUP, GROUP)
        kk = jnp.where(klow, k_ref[n, pl.ds(off, MOBA_BLOCK), :], f_own)
        vv = jnp.where(klow, v_ref[n, pl.ds(off, MOBA_BLOCK), :], one)
        s = _nt_dot(qa_ref[hs].reshape(rows, SLAB), kk)
        s = jnp.where(causal, s, NEG_INF)
        m0 = jnp.max(s, axis=1, keepdims=True)
        p = jnp.exp(s - m0)
        m_ref[hs] = jnp.broadcast_to(m0, (rows, SLAB)).reshape(GROUP, tq, SLAB)
        acc_ref[hs] = jnp.dot(p.astype(_BF16), vv, preferred_element_type=_F32).reshape(GROUP, tq, SLAB)

    def past(c, _):
        offc = pl.multiple_of(c * MOBA_BLOCK, MOBA_BLOCK)
        f_c = key_aux(((c - i) * MOBA_BLOCK).astype(_F32), c)
        for n in range(N_KV_HEADS):
            hs = pl.ds(n * GROUP, GROUP)
            kk = jnp.where(klow, k_ref[n, pl.ds(offc, MOBA_BLOCK), :], f_c)
            vv = jnp.where(klow, v_ref[n, pl.ds(offc, MOBA_BLOCK), :], one)
            s = _nt_dot(qa_ref[hs].reshape(rows, SLAB), kk)
            m_old = m_ref[hs].reshape(rows, SLAB)
            m_new = jnp.maximum(m_old, jnp.max(s, axis=1, keepdims=True))
            alpha = jnp.exp(m_old - m_new)
            p = jnp.exp(s - jnp.concatenate([m_new, m_new], axis=1))
            pv = jnp.dot(p.astype(_BF16), vv, preferred_element_type=_F32)
            m_ref[hs] = m_new.reshape(GROUP, tq, SLAB)
            acc_ref[hs] = (alpha * acc_ref[hs].reshape(rows, SLAB) + pv).reshape(GROUP, tq, SLAB)
        return 0

    lax.fori_loop(0, i, past, 0)
    _pair_heads(acc_ref, o_ref, normalise=True)


def _moba_prompt(stab, q, ks, vs, kmtab, *, batch, seq):
    tq = TQ
    nq = seq // tq
    return pl.pallas_call(
        functools.partial(_moba_prompt_body, tq=tq),
        grid=(batch, nq),
        in_specs=[
            pl.BlockSpec(stab.shape, lambda b, i: (0, 0, 0)),
            pl.BlockSpec((N_HEADS, tq, SLAB), lambda b, i: (0, b * nq + i, 0)),
            pl.BlockSpec((N_KV_HEADS, seq, SLAB), lambda b, i: (0, b, 0)),
            pl.BlockSpec((N_KV_HEADS, seq, SLAB), lambda b, i: (0, b, 0)),
            pl.BlockSpec((None, N_KV_HEADS, GATE_LANES, SLAB), lambda b, i: (b, 0, 0, 0)),
        ],
        out_specs=pl.BlockSpec((tq, D_MODEL), lambda b, i: (b * nq + i, 0)),
        out_shape=jax.ShapeDtypeStruct((batch * seq, D_MODEL), _BF16),
        scratch_shapes=[pltpu.VMEM((N_HEADS, tq, SLAB), _BF16),
                        pltpu.VMEM((N_HEADS, tq, SLAB), _F32),
                        pltpu.VMEM((N_HEADS, tq, SLAB), _F32)],
        compiler_params=pltpu.CompilerParams(dimension_semantics=("arbitrary", "arbitrary"),
                                             vmem_limit_bytes=VMEM_LIMIT),
        name="moba_prompt",
    )(stab, q, ks, vs, kmtab)


def _stick_prompt_body(q_ref, k_ref, v_ref, o_ref, qa_ref, acc_ref, carry_ref, *, tq):
    i = pl.program_id(1)
    rows = GROUP * tq
    low = lax.broadcasted_iota(jnp.int32, q_ref.shape, 2) < HEAD_DIM
    qa_ref[...] = jnp.where(low, q_ref[...] * SCALE, 0.0).astype(_BF16)
    acc_ref[...] = jnp.zeros(acc_ref.shape, _F32)
    carry_ref[...] = jnp.zeros(carry_ref.shape, _F32)
    tri = _stick_tri()
    qrow = lax.broadcasted_iota(jnp.int32, (rows, STICK_KB), 0) & (tq - 1)
    kcol = lax.broadcasted_iota(jnp.int32, (rows, STICK_KB), 1)
    rel = qrow - kcol
    last_chunk = (i + 1) * (tq // STICK_KB) - 1

    def cond(state):
        c, alive = state
        return jnp.logical_and(c >= 0, alive > 0)

    def step(state):
        c, _ = state
        off = pl.multiple_of(c * STICK_KB, STICK_KB)
        causal = rel + (i * tq - c * STICK_KB) > 0
        top = jnp.float32(-jnp.inf)
        for n in range(N_KV_HEADS):
            hs = pl.ds(n * GROUP, GROUP)
            z = _nt_dot(qa_ref[hs].reshape(rows, SLAB), k_ref[n, pl.ds(off, STICK_KB), :])
            pv, carry = _stick_chunk(z, causal, carry_ref[hs].reshape(rows, STICK_KB), tri,
                                     vc=v_ref[n, pl.ds(off, STICK_KB), :])
            acc_ref[hs] += pv.reshape(GROUP, tq, SLAB)
            carry_ref[hs] = carry.reshape(GROUP, tq, STICK_KB)
            top = jnp.maximum(top, jnp.max(carry))
        return c - 1, (top > STICK_DEAD).astype(jnp.int32)

    lax.while_loop(cond, step, (last_chunk, jnp.int32(1)))
    _pair_heads(acc_ref, o_ref, normalise=False)


def _stick_prompt(q, ks, vs, *, batch, seq):
    tq = TQ
    nq = seq // tq
    return pl.pallas_call(
        functools.partial(_stick_prompt_body, tq=tq),
        grid=(batch, nq),
        in_specs=[
            pl.BlockSpec((N_HEADS, tq, SLAB), lambda b, i: (0, b * nq + i, 0)),
            pl.BlockSpec((N_KV_HEADS, seq, SLAB), lambda b, i: (0, b, 0)),
            pl.BlockSpec((N_KV_HEADS, seq, SLAB), lambda b, i: (0, b, 0)),
        ],
        out_specs=pl.BlockSpec((tq, D_MODEL), lambda b, i: (b * nq + i, 0)),
        out_shape=jax.ShapeDtypeStruct((batch * seq, D_MODEL), _BF16),
        scratch_shapes=[pltpu.VMEM((N_HEADS, tq, SLAB), _BF16),
                        pltpu.VMEM((N_HEADS, tq, SLAB), _F32),
                        pltpu.VMEM((N_HEADS, tq, STICK_KB), _F32)],
        compiler_params=pltpu.CompilerParams(dimension_semantics=("arbitrary", "arbitrary"),
                                             vmem_limit_bytes=VMEM_LIMIT),
        name="stick_prompt",
    )(q, ks, vs)


def _sample_rows(q_ref):
    q = q_ref[0] * SCALE
    grp = _lane_group(q.shape)
    return jnp.concatenate([jnp.where(grp == n, q, 0.0) for n in range(N_KV_HEADS)], axis=0)


def _sample_out(acc):
    r = acc.shape[0] // N_KV_HEADS
    grp = _lane_group((r, KV_DIM))
    out = jnp.zeros((r, KV_DIM), _F32)
    for n in range(N_KV_HEADS):
        out = jnp.where(grp == n, acc[n * r:(n + 1) * r], out)
    return out


def _moba_sample_body(pt_ref, slope_ref, qpos_ref, q_ref, kown_ref, vown_ref, *rest, n_steps):
    kp = rest[:PAGES_PER_STEP]
    vp = rest[PAGES_PER_STEP:2 * PAGES_PER_STEP]
    o_ref, oall_ref, mtab_ref, ltab_ref, kmt_ref = rest[2 * PAGES_PER_STEP:]
    sp = pl.program_id(1)
    qm = _sample_rows(q_ref)
    qb = qm.astype(_BF16)
    rows = qm.shape[0]
    slope = slope_ref[...]
    qpos = qpos_ref[...]
    gcol = lax.broadcasted_iota(jnp.int32, (rows, GATE_LANES), 1)
    kmcol = lax.broadcasted_iota(jnp.int32, (KV_DIM, GATE_LANES), 1)
    kcol = lax.broadcasted_iota(jnp.int32, (rows, MOBA_BLOCK), 1).astype(_F32)
    pages_per_block = MOBA_BLOCK // PAGE_SIZE
    blocks_per_step = PAGES_PER_STEP // pages_per_block

    @pl.when(sp == 0)
    def _():
        for tab in (mtab_ref, ltab_ref, kmt_ref):
            tab[...] = jnp.zeros(tab.shape, _F32)

    for jb in range(blocks_per_step):
        m = sp * blocks_per_step + jb
        kblk = jnp.concatenate([kp[jb * pages_per_block + u][...] for u in range(pages_per_block)], axis=1)
        vblk = jnp.concatenate([vp[jb * pages_per_block + u][...] for u in range(pages_per_block)], axis=1)
        ksum = jnp.sum(kblk, axis=1, keepdims=True)
        kmt_ref[...] = jnp.where(kmcol == m, ksum * (1.0 / MOBA_BLOCK), kmt_ref[...])
        kpos = kcol + (m * MOBA_BLOCK).astype(_F32)
        s = jnp.dot(qb, kblk.astype(_BF16), preferred_element_type=_F32) - slope * (qpos - kpos)
        mb = jnp.max(s, axis=1, keepdims=True)
        p = jnp.exp(s - mb)
        oall_ref[m] = _nt_dot(p.astype(_BF16), vblk.astype(_BF16))
        here = gcol == m
        mtab_ref[...] = jnp.where(here, mb, mtab_ref[...])
        ltab_ref[...] = jnp.where(here, jnp.sum(p, axis=1, keepdims=True), ltab_ref[...])

    @pl.when(sp == n_steps - 1)
    def _():
        n_blocks = n_steps * blocks_per_step
        ocol = lax.broadcasted_iota(jnp.int32, (rows, PAGE_SIZE), 1).astype(_F32)
        dist = (qpos - (n_blocks * MOBA_BLOCK)) - ocol
        s = _nt_dot(qb, kown_ref[0]) - slope * dist
        s = jnp.where(dist >= 0.0, s, NEG_INF)
        m_own = jnp.max(s, axis=1, keepdims=True)
        p = jnp.exp(s - m_own)
        l_own = jnp.sum(p, axis=1, keepdims=True)
        o_own = jnp.dot(p.astype(_BF16), vown_ref[0], preferred_element_type=_F32)

        gate = jnp.dot(qm, kmt_ref[...], precision=lax.Precision.HIGHEST, preferred_element_type=_F32)
        picked = _top3_select(gate, gcol < n_blocks, 1) > 0.0
        mtab = mtab_ref[...]
        m_all = jnp.maximum(m_own, jnp.max(jnp.where(picked, mtab, NEG_INF), axis=1, keepdims=True))
        w = jnp.where(picked, jnp.exp(mtab - m_all), 0.0)
        w_own = jnp.exp(m_own - m_all)
        l_all = w_own * l_own + jnp.sum(w * ltab_ref[...], axis=1, keepdims=True)

        def merge(mi, acc):
            wm = jnp.sum(jnp.where(gcol == mi, w, 0.0), axis=1, keepdims=True)
            return acc + wm * oall_ref[mi]

        acc = lax.fori_loop(0, n_blocks, merge, w_own * o_own)
        o_ref[0] = _sample_out(acc / l_all).astype(o_ref.dtype)


def _page_specs(layer, page_of):
    def spec(j):
        return pl.BlockSpec((None, None, KV_DIM, PAGE_SIZE),
                            lambda b, s, pt: (layer, pt[b, page_of(s, j)], 0, 0))
    return [spec(j) for j in range(PAGES_PER_STEP)] * 2


def _moba_sample(page_table, slope_rows, qpos_rows, q_rows, k_own, v_own, cache_kt, cache_vt, *, layer):
    dec_b, n_pages = page_table.shape
    n_steps = n_pages // PAGES_PER_STEP
    rows = q_rows.shape[1] * N_KV_HEADS
    n_blocks = n_pages * PAGE_SIZE // MOBA_BLOCK
    grid_spec = pltpu.PrefetchScalarGridSpec(
        num_scalar_prefetch=1,
        grid=(dec_b, n_steps),
        in_specs=[
            pl.BlockSpec((rows, 1), lambda b, s, pt: (0, 0)),
            pl.BlockSpec((rows, 1), lambda b, s, pt: (0, 0)),
            pl.BlockSpec((1, q_rows.shape[1], KV_DIM), lambda b, s, pt: (b, 0, 0)),
            pl.BlockSpec((1, PAGE_SIZE, KV_DIM), lambda b, s, pt: (b, 0, 0)),
            pl.BlockSpec((1, PAGE_SIZE, KV_DIM), lambda b, s, pt: (b, 0, 0)),
        ] + _page_specs(layer, lambda s, j: s * PAGES_PER_STEP + j),
        out_specs=pl.BlockSpec((1, q_rows.shape[1], KV_DIM), lambda b, s, pt: (b, 0, 0)),
        scratch_shapes=[pltpu.VMEM((n_blocks, rows, KV_DIM), _F32),
                        pltpu.VMEM((rows, GATE_LANES), _F32),
                        pltpu.VMEM((rows, GATE_LANES), _F32),
                        pltpu.VMEM((KV_DIM, GATE_LANES), _F32)],
    )
    return pl.pallas_call(
        functools.partial(_moba_sample_body, n_steps=n_steps),
        grid_spec=grid_spec,
        out_shape=jax.ShapeDtypeStruct(q_rows.shape, _BF16),
        compiler_params=pltpu.CompilerParams(dimension_semantics=("arbitrary", "arbitrary"),
                                             vmem_limit_bytes=VMEM_LIMIT),
        name="moba_sample",
    )(page_table, slope_rows, qpos_rows, q_rows, k_own, v_own,
      *([cache_kt] * PAGES_PER_STEP), *([cache_vt] * PAGES_PER_STEP))


def _stick_sample_body(pt_ref, tok_ref, q_ref, kown_ref, vown_ref, *rest, n_steps):
    kp = rest[:PAGES_PER_STEP]
    vp = rest[PAGES_PER_STEP:2 * PAGES_PER_STEP]
    o_ref, acc_ref, carry_ref, alive_ref = rest[2 * PAGES_PER_STEP:]
    sp = pl.program_id(1)
    qb = _sample_rows(q_ref).astype(_BF16)
    rows = qb.shape[0]
    tri = _stick_tri()

    @pl.when(sp == 0)
    def _():
        kcol = lax.broadcasted_iota(jnp.int32, (rows, STICK_KB), 1).astype(_F32)
        causal = kcol < tok_ref[...]
        z = _nt_dot(qb, kown_ref[0])
        pv, carry = _stick_chunk(z, causal, jnp.zeros((rows, STICK_KB), _F32), tri, vc=vown_ref[0])
        acc_ref[...] = pv
        carry_ref[...] = carry
        alive_ref[0] = (jnp.max(carry) > STICK_DEAD).astype(jnp.int32)

    for j in range(PAGES_PER_STEP):
        @pl.when(alive_ref[0] > 0)
        def _():
            z = jnp.dot(qb, kp[j][...].astype(_BF16), preferred_element_type=_F32)
            pv, carry = _stick_chunk(z, None, carry_ref[...], tri, vc_nt=vp[j][...].astype(_BF16))
            acc_ref[...] += pv
            carry_ref[...] = carry
            alive_ref[0] = (jnp.max(carry) > STICK_DEAD).astype(jnp.int32)

    @pl.when(sp == n_steps - 1)
    def _():
        o_ref[0] = _sample_out(acc_ref[...]).astype(o_ref.dtype)


def _stick_sample(page_table, tok_rows, q_rows, k_own, v_own, cache_kt, cache_vt, *, layer):
    dec_b, n_pages = page_table.shape
    n_steps = n_pages // PAGES_PER_STEP
    rows = q_rows.shape[1] * N_KV_HEADS
    grid_spec = pltpu.PrefetchScalarGridSpec(
        num_scalar_prefetch=1,
        grid=(dec_b, n_steps),
        in_specs=[
            pl.BlockSpec((rows, 1), lambda b, s, pt: (0, 0)),
            pl.BlockSpec((1, q_rows.shape[1], KV_DIM), lambda b, s, pt: (b, 0, 0)),
            pl.BlockSpec((1, PAGE_SIZE, KV_DIM), lambda b, s, pt: (b, 0, 0)),
            pl.BlockSpec((1, PAGE_SIZE, KV_DIM), lambda b, s, pt: (b, 0, 0)),
        ] + _page_specs(layer, lambda s, j: n_pages - 1 - (s * PAGES_PER_STEP + j)),
        out_specs=pl.BlockSpec((1, q_rows.shape[1], KV_DIM), lambda b, s, pt: (b, 0, 0)),
        scratch_shapes=[pltpu.VMEM((rows, KV_DIM), _F32),
                        pltpu.VMEM((rows, STICK_KB), _F32),
                        pltpu.SMEM((1,), jnp.int32)],
    )
    return pl.pallas_call(
        functools.partial(_stick_sample_body, n_steps=n_steps),
        grid_spec=grid_spec,
        out_shape=jax.ShapeDtypeStruct(q_rows.shape, _BF16),
        compiler_params=pltpu.CompilerParams(dimension_semantics=("arbitrary", "arbitrary"),
                                             vmem_limit_bytes=VMEM_LIMIT),
        name="stick_sample",
    )(page_table, tok_rows, q_rows, k_own, v_own,
      *([cache_kt] * PAGES_PER_STEP), *([cache_vt] * PAGES_PER_STEP))


def _ffn_body(*refs, tm, tiles_per_seq, sample, final):
    it = iter(refs)
    x_ref, o_ref, wo_ref, g_ref, wup_ref, cw_ref, cb_ref, wdn_ref = (next(it) for _ in range(8))
    s1_ref = s2_ref = tok_ref = gfin_ref = None
    if sample:
        s1_ref, s2_ref, tok_ref = next(it), next(it), next(it)
    if final:
        gfin_ref = next(it)
    xo_ref, u_ref = next(it), next(it)
    y_ref = next(it) if final else None
    ubuf, carry, acc_ref = next(it), next(it), next(it)

    i = pl.program_id(0)
    x1 = x_ref[...] + jnp.dot(o_ref[...], wo_ref[...], preferred_element_type=_F32)
    ms = jnp.mean(x1 * x1, axis=-1, keepdims=True)
    h = ((x1 * lax.rsqrt(ms + RMS_EPS)) * g_ref[...]).astype(_BF16)

    if not sample:
        @pl.when(i % tiles_per_seq == 0)
        def _():
            carry[...] = jnp.zeros(carry.shape, _F32)
    acc_ref[...] = jnp.zeros(acc_ref.shape, _F32)
    ubuf[0:8, :] = jnp.zeros((8, FF_CHUNK), _F32)

    def conv(jj):
        u = jnp.dot(h, wup_ref[jj], preferred_element_type=_F32)
        ubuf[8:8 + tm, :] = u
        if sample:
            u_ref[jj] = u
            tok = tok_ref[...]
            u1 = jnp.where(tok < 1.0, s1_ref[jj], ubuf[7:7 + tm, :])
            u2 = jnp.where(tok < 2.0, s2_ref[jj], ubuf[6:6 + tm, :])
        else:
            ubuf[6:8, :] = carry[jj, 6:8, :]
            u1 = ubuf[7:7 + tm, :]
            u2 = ubuf[6:6 + tm, :]
            carry[jj, 6:8, :] = u[tm - 2:tm, :]
        w = cw_ref[jj]
        return cb_ref[jj] + w[0:1, :] * u2 + w[1:2, :] * u1 + w[2:3, :] * u

    def chunk(j, _):
        cg = conv(j)
        cv = conv(j + N_FF_CHUNKS)
        act = (cg / (1.0 + jnp.exp(-cg))) * cv
        acc_ref[...] += jnp.dot(act.astype(_BF16), wdn_ref[j], preferred_element_type=_F32)
        return 0

    lax.fori_loop(0, N_FF_CHUNKS, chunk, 0)
    xo = x1 + acc_ref[...]
    xo_ref[...] = xo
    if not sample:
        u_ref[...] = carry[...]
    if final:
        ms2 = jnp.mean(xo * xo, axis=-1, keepdims=True)
        y_ref[...] = (xo * lax.rsqrt(ms2 + RMS_EPS)) * gfin_ref[...]


def _oproj_ffn(x, o, wo, g, wup, cw, cb, wdn, *, tm, seq, sample_state=None, g_final=None):
    t = x.shape[0]
    sample = sample_state is not None
    final = g_final is not None
    n_tiles = t // tm
    tiles_per_seq = max(seq // tm, 1)
    const2 = lambda i: (0, 0)
    const3 = lambda i: (0, 0, 0)
    once = pl.Buffered(1)
    in_specs = [
        pl.BlockSpec((tm, D_MODEL), lambda i: (i, 0)),
        pl.BlockSpec((tm, D_MODEL), lambda i: (i, 0)),
        pl.BlockSpec(wo.shape, const2, pipeline_mode=once),
        pl.BlockSpec((1, D_MODEL), const2),
        pl.BlockSpec(wup.shape, const3, pipeline_mode=once),
        pl.BlockSpec(cw.shape, const3),
        pl.BlockSpec(cb.shape, const3),
        pl.BlockSpec(wdn.shape, const3, pipeline_mode=once),
    ]
    args = [x, o, wo, g, wup, cw, cb, wdn]
    if sample:
        s1, s2, tok = sample_state
        in_specs += [pl.BlockSpec(s1.shape, const3), pl.BlockSpec(s2.shape, const3),
                     pl.BlockSpec(tok.shape, const2)]
        args += [s1, s2, tok]
    if final:
        in_specs.append(pl.BlockSpec((1, D_MODEL), const2))
        args.append(g_final)
    out_shape = [jax.ShapeDtypeStruct((t, D_MODEL), _F32)]
    out_specs = [pl.BlockSpec((tm, D_MODEL), lambda i: (i, 0))]
    if sample:
        out_shape.append(jax.ShapeDtypeStruct((2 * N_FF_CHUNKS, t, FF_CHUNK), _F32))
        out_specs.append(pl.BlockSpec((2 * N_FF_CHUNKS, tm, FF_CHUNK), lambda i: (0, i, 0)))
    else:
        n_seq = t // seq
        out_shape.append(jax.ShapeDtypeStruct((n_seq, 2 * N_FF_CHUNKS, 8, FF_CHUNK), _F32))
        out_specs.append(pl.BlockSpec((None, 2 * N_FF_CHUNKS, 8, FF_CHUNK),
                                      lambda i: (i // tiles_per_seq, 0, 0, 0)))
    if final:
        out_shape.append(jax.ShapeDtypeStruct((t, D_MODEL), _F32))
        out_specs.append(pl.BlockSpec((tm, D_MODEL), lambda i: (i, 0)))
    return pl.pallas_call(
        functools.partial(_ffn_body, tm=tm, tiles_per_seq=tiles_per_seq, sample=sample, final=final),
        grid=(n_tiles,),
        in_specs=in_specs,
        out_specs=out_specs,
        out_shape=out_shape,
        scratch_shapes=[pltpu.VMEM((tm + 8, FF_CHUNK), _F32),
                        pltpu.VMEM((2 * N_FF_CHUNKS, 8, FF_CHUNK), _F32),
                        pltpu.VMEM((tm, D_MODEL), _F32)],
        compiler_params=pltpu.CompilerParams(dimension_semantics=("arbitrary",),
                                             vmem_limit_bytes=VMEM_LIMIT),
        name="oproj_ffn_sample" if sample else "oproj_ffn",
    )(*args)


def _chunk_cols(a):
    lead = a.shape[:-2]
    r = a.shape[-2]
    a = a.reshape(lead + (r, 2 * N_FF_CHUNKS, FF_CHUNK))
    return jnp.moveaxis(a, -2, -3)


def _unchunk_cols(a):
    a = jnp.moveaxis(a, -3, -2)
    return a.reshape(a.shape[:-2] + (2 * D_FF,))


def _swap_halves(w):
    s = w.shape
    return w.reshape(s[:-1] + (s[-1] // SLAB, 2, HEAD_DIM))[..., ::-1, :].reshape(s)


def _bf16_pieces(x):
    hi = x.astype(_BF16).astype(_F32)
    mid = (x - hi).astype(_BF16).astype(_F32)
    lo = (x - hi - mid).astype(_BF16).astype(_F32)
    return hi, mid, lo


def kernel(x_prompt, x_sample, cache_k, cache_v, state_conv, page_table, g_attn, w_qkv, w_o,
           g_ffn, w_up, conv_w, conv_b, w_down, g_final):
    depth = w_qkv.shape[0]
    batch, seq, _ = x_prompt.shape
    dec_b, dec_s, _ = x_sample.shape
    n_pool = cache_k.shape[1]
    n_pages = page_table.shape[1]
    past = n_pages * PAGE_SIZE
    tp = batch * seq
    ts = dec_b * dec_s
    nb = seq // MOBA_BLOCK

    wq, wk, wv = (w_qkv[:, :, :D_MODEL], w_qkv[:, :, D_MODEL:D_MODEL + KV_DIM],
                  w_qkv[:, :, D_MODEL + KV_DIM:])
    wqkv = jnp.concatenate([wq, _swap_halves(wq), wk, wv, _swap_halves(wk), _swap_halves(wv)],
                           axis=2).astype(_BF16)
    wo = w_o.astype(_BF16)
    wup = _chunk_cols(w_up).astype(_BF16)
    cw = _chunk_cols(conv_w)
    cb = _chunk_cols(conv_b[:, None, :])
    wdn = w_down.reshape(depth, N_FF_CHUNKS, FF_CHUNK, D_MODEL).astype(_BF16)
    g_attn2 = g_attn[:, None, :]
    g_ffn2 = g_ffn[:, None, :]
    g_fin2 = g_final[None, :]

    hidx = jnp.arange(1, N_HEADS + 1, dtype=_F32)
    slopes = jnp.exp2(-8.0 * hidx / N_HEADS)
    pieces = jnp.stack(_bf16_pieces(slopes) * 2, axis=1)
    stab = jnp.zeros((N_HEADS, 8, SLAB), _F32).at[:, :, SLOPE_LANE0:SLOPE_LANE0 + 6].set(pieces[:, None, :])
    slope_rows = jnp.repeat(slopes, dec_s)[:, None]
    tok_rows = jnp.tile(jnp.arange(dec_s, dtype=_F32), N_HEADS)[:, None]
    qpos_rows = tok_rows + float(past)
    tok_seq = jnp.tile(jnp.arange(dec_s, dtype=_F32), dec_b)[:, None]

    cache_kt = cache_k.transpose(0, 1, 3, 4, 2).reshape(depth, n_pool, KV_DIM, PAGE_SIZE)
    cache_vt = cache_v.transpose(0, 1, 3, 4, 2).reshape(depth, n_pool, KV_DIM, PAGE_SIZE)

    st = state_conv
    zero = jnp.zeros_like(st[:, :, :1])
    s1 = jnp.concatenate([st[:, :, 1:2], zero, zero, zero][:dec_s], axis=2)
    s2 = jnp.concatenate([st[:, :, 0:1], st[:, :, 1:2], zero, zero][:dec_s], axis=2)
    s1 = _chunk_cols(s1.reshape(depth, ts, 2 * D_FF))
    s2 = _chunk_cols(s2.reshape(depth, ts, 2 * D_FF))

    xp = x_prompt.reshape(tp, D_MODEL)
    xs = x_sample.reshape(ts, D_MODEL)
    kp_l, vp_l, ks_l, vs_l, cp_l, cs_l = [], [], [], [], [], []
    yp = ys = None
    pad_own = jnp.zeros((dec_b, PAGE_SIZE - dec_s, KV_DIM), _BF16)
    for l in range(depth):
        last = l == depth - 1
        gfin = g_fin2 if last else None
        moba = l % 2 == 0
        q, k, v, kslab, vslab, *km = _rms_qkv(xp, g_attn2[l], wqkv[l], tm=512, with_kmean=moba)
        if moba:
            kmt = km[0].reshape(batch, nb, N_KV_HEADS, HEAD_DIM).transpose(0, 2, 1, 3)
            kmt = jnp.pad(kmt, ((0, 0), (0, 0), (FLAG_LANE0, GATE_LANES - FLAG_LANE0 - nb),
                                (0, SLAB - HEAD_DIM)))
            o = _moba_prompt(stab, q, kslab, vslab, kmt, batch=batch, seq=seq)
        else:
            o = _stick_prompt(q, kslab, vslab, batch=batch, seq=seq)
        outs = _oproj_ffn(xp, o, wo[l], g_ffn2[l], wup[l], cw[l], cb[l], wdn[l],
                          tm=512 if last else 1024, seq=seq, g_final=gfin)
        xp, cst = outs[0], outs[1]
        if last:
            yp = outs[2]
        kp_l.append(k.reshape(batch, seq, N_KV_HEADS, HEAD_DIM))
        vp_l.append(v.reshape(batch, seq, N_KV_HEADS, HEAD_DIM))
        cp_l.append(_unchunk_cols(cst[:, :, 6:8, :]))
        qs, ksn, vsn, _, _ = _rms_qkv(xs, g_attn2[l], wqkv[l], tm=ts, with_kmean=False)
        qnat = qs[:, :, :HEAD_DIM].reshape(N_KV_HEADS, GROUP, dec_b, dec_s, HEAD_DIM)
        q_rows = qnat.transpose(2, 1, 3, 0, 4).reshape(dec_b, GROUP * dec_s, KV_DIM)
        k_own = jnp.concatenate([ksn.astype(_BF16).reshape(dec_b, dec_s, KV_DIM), pad_own], axis=1)
        v_own = jnp.concatenate([vsn.astype(_BF16).reshape(dec_b, dec_s, KV_DIM), pad_own], axis=1)
        if moba:
            o_rows = _moba_sample(page_table, slope_rows, qpos_rows, q_rows, k_own, v_own,
                                  cache_kt, cache_vt, layer=l)
        else:
            o_rows = _stick_sample(page_table, tok_rows, q_rows, k_own, v_own, cache_kt, cache_vt, layer=l)
        os_ = o_rows.reshape(dec_b, GROUP, dec_s, N_KV_HEADS, HEAD_DIM).transpose(0, 2, 3, 1, 4)
        outs = _oproj_ffn(xs, os_.reshape(ts, D_MODEL), wo[l], g_ffn2[l], wup[l], cw[l], cb[l], wdn[l],
                          tm=ts, seq=dec_s, sample_state=(s1[l], s2[l], tok_seq), g_final=gfin)
        xs, u_all = outs[0], outs[1]
        if last:
            ys = outs[2]
        ks_l.append(ksn.reshape(dec_b, dec_s, N_KV_HEADS, HEAD_DIM))
        vs_l.append(vsn.reshape(dec_b, dec_s, N_KV_HEADS, HEAD_DIM))
        u_full = _unchunk_cols(u_all).reshape(dec_b, dec_s, 2 * D_FF)
        cs_l.append(u_full[:, dec_s - (CONV_W - 1):, :])
    return (yp.reshape(batch, seq, D_MODEL), ys.reshape(dec_b, dec_s, D_MODEL),
            jnp.stack(kp_l), jnp.stack(vp_l), jnp.stack(ks_l), jnp.stack(vs_l),
            jnp.stack(cp_l), jnp.stack(cs_l))
```

```python
import functools

import jax
import jax.numpy as jnp
from jax import lax
from jax.experimental import pallas as pl
from jax.experimental.pallas import tpu as pltpu

D_MODEL = 1024
N_HEADS = 16
HEAD_DIM = 64
N_KV_HEADS = 4
GROUP = N_HEADS // N_KV_HEADS
KV_DIM = N_KV_HEADS * HEAD_DIM
SLAB = 2 * HEAD_DIM
MOBA_BLOCK = 256
MOBA_TOPK = 3
D_FF = 2816
CONV_W = 3
PAGE_SIZE = 128
RMS_EPS = 1e-6
NEG_INF = -1e30
SCALE = HEAD_DIM ** -0.5

FF_CHUNK = 256
N_FF_CHUNKS = D_FF // FF_CHUNK
STICK_KB = 128
LOG2E = 1.4426950408889634
STICK_DEAD = -110.0 * LOG2E
MOBA_DEAD = 152.0
NORM_SLACK = 1.04
GATE_LANES = 128
FLAG_LANE0 = HEAD_DIM
SLOPE_LANE0 = 96
PAGES_PER_STEP = 8
VMEM_LIMIT = 56 * 1024 * 1024
TQ = 256
STICK_TQ = 256

_BF16 = jnp.bfloat16
_F32 = jnp.float32


def _nt_dot(a, b):
    return lax.dot_general(a, b, (((1,), (1,)), ((), ())), preferred_element_type=_F32)


def _lane_group(shape):
    return lax.broadcasted_iota(jnp.int32, shape, len(shape) - 1) >> 6


def _qkv_body(x_ref, g_ref, w_ref, q_ref, k_ref, v_ref, ks_ref, vs_ref, *km_ref, tm):
    x = x_ref[...]
    ms = jnp.mean(x * x, axis=-1, keepdims=True)
    h = (x * lax.rsqrt(ms + RMS_EPS)) * g_ref[...]
    qkv = jnp.dot(h.astype(_BF16), w_ref[...], preferred_element_type=_F32)
    for j in range(N_HEADS // 2):
        q_ref[2 * j] = qkv[:, j * SLAB:(j + 1) * SLAB]
        q_ref[2 * j + 1] = qkv[:, D_MODEL + j * SLAB:D_MODEL + (j + 1) * SLAB]
    base = 2 * D_MODEL
    k = qkv[:, base:base + KV_DIM]
    v = qkv[:, base + KV_DIM:base + 2 * KV_DIM]
    k_sw = qkv[:, base + 2 * KV_DIM:base + 3 * KV_DIM]
    v_sw = qkv[:, base + 3 * KV_DIM:base + 4 * KV_DIM]
    k_ref[...] = k
    v_ref[...] = v
    for j in range(N_KV_HEADS // 2):
        ks_ref[2 * j] = k[:, j * SLAB:(j + 1) * SLAB].astype(_BF16)
        ks_ref[2 * j + 1] = k_sw[:, j * SLAB:(j + 1) * SLAB].astype(_BF16)
        vs_ref[2 * j] = v[:, j * SLAB:(j + 1) * SLAB].astype(_BF16)
        vs_ref[2 * j + 1] = v_sw[:, j * SLAB:(j + 1) * SLAB].astype(_BF16)
    if km_ref:
        li = lax.broadcasted_iota(jnp.int32, (KV_DIM, KV_DIM), 0) >> 6
        lj = lax.broadcasted_iota(jnp.int32, (KV_DIM, KV_DIM), 1) >> 6
        k2 = jnp.dot((k * k).astype(_BF16), jnp.where(li == lj, 1.0, 0.0).astype(_BF16),
                     preferred_element_type=_F32)
        for j in range(tm // MOBA_BLOCK):
            blk = slice(j * MOBA_BLOCK, (j + 1) * MOBA_BLOCK)
            km_ref[0][j] = jnp.mean(k[blk], axis=0, keepdims=True)
            km_ref[1][j] = jnp.max(k2[blk], axis=0, keepdims=True)


def _rms_qkv(x, g, w, *, tm, with_kmean):
    t = x.shape[0]
    out_shape = [
        jax.ShapeDtypeStruct((N_HEADS, t, SLAB), _F32),
        jax.ShapeDtypeStruct((t, KV_DIM), _F32),
        jax.ShapeDtypeStruct((t, KV_DIM), _F32),
        jax.ShapeDtypeStruct((N_KV_HEADS, t, SLAB), _BF16),
        jax.ShapeDtypeStruct((N_KV_HEADS, t, SLAB), _BF16),
    ]
    row = pl.BlockSpec((tm, KV_DIM), lambda i: (i, 0))
    slab = pl.BlockSpec((N_KV_HEADS, tm, SLAB), lambda i: (0, i, 0))
    out_specs = [pl.BlockSpec((N_HEADS, tm, SLAB), lambda i: (0, i, 0)), row, row, slab, slab]
    if with_kmean:
        for _ in range(2):
            out_shape.append(jax.ShapeDtypeStruct((t // MOBA_BLOCK, 1, KV_DIM), _F32))
            out_specs.append(pl.BlockSpec((tm // MOBA_BLOCK, 1, KV_DIM), lambda i: (i, 0, 0)))
    return pl.pallas_call(
        functools.partial(_qkv_body, tm=tm),
        grid=(t // tm,),
        in_specs=[
            pl.BlockSpec((tm, D_MODEL), lambda i: (i, 0)),
            pl.BlockSpec((1, D_MODEL), lambda i: (0, 0)),
            pl.BlockSpec(w.shape, lambda i: (0, 0)),
        ],
        out_specs=out_specs,
        out_shape=out_shape,
        compiler_params=pltpu.CompilerParams(dimension_semantics=("arbitrary",),
                                             vmem_limit_bytes=VMEM_LIMIT),
        name="rms_qkv",
    )(x, g, w)


def _top3_select(gate, eligible, axis):
    idx = lax.broadcasted_iota(jnp.int32, gate.shape, axis).astype(_F32)
    neg = jnp.float32(-jnp.inf)
    gm = jnp.where(eligible, gate, neg)
    sel = jnp.zeros(gate.shape, _F32)
    for _ in range(MOBA_TOPK):
        mx = jnp.max(gm, axis=axis, keepdims=True)
        first = jnp.min(jnp.where(gm == mx, idx, jnp.float32(1e9)), axis=axis, keepdims=True)
        pick = jnp.logical_and(idx == first, mx > neg)
        sel = jnp.where(pick, 1.0, sel)
        gm = jnp.where(pick, neg, gm)
    return sel


def _stick_tri():
    j = lax.broadcasted_iota(jnp.int32, (2 * STICK_KB, 2 * STICK_KB), 0) & (STICK_KB - 1)
    s = lax.broadcasted_iota(jnp.int32, (2 * STICK_KB, 2 * STICK_KB), 1)
    return jnp.where(jnp.logical_or(s >= STICK_KB, j > s), 1.0, 0.0).astype(_BF16)


def _stick_chunk(z, causal, carry, tri, vc_nt=None, vc=None):
    nz = -z
    e = jnp.exp2(jnp.minimum(z, nz))
    log_keep = jnp.minimum(nz, 0.0) - jnp.log2(1.0 + e)
    log_sig = z + log_keep
    if causal is not None:
        log_keep = jnp.where(causal, log_keep, 0.0)
    hi = log_keep.astype(_BF16)
    lo = (log_keep - hi.astype(_F32)).astype(_BF16)
    ct = jnp.dot(jnp.concatenate([hi, lo], axis=1), tri, preferred_element_type=_F32)
    log_after = carry + ct[:, :STICK_KB]
    a = jnp.exp2(log_sig + log_after)
    if causal is not None:
        a = jnp.where(causal, a, 0.0)
    a = a.astype(_BF16)
    pv = _nt_dot(a, vc_nt) if vc is None else jnp.dot(a, vc, preferred_element_type=_F32)
    return pv, carry + ct[:, STICK_KB:]


def _pair_heads(acc_ref, o_ref, normalise):
    low = lax.broadcasted_iota(jnp.int32, acc_ref.shape[1:], 1) < HEAD_DIM
    for j in range(N_HEADS // 2):
        a0 = acc_ref[2 * j]
        a1 = acc_ref[2 * j + 1]
        if normalise:
            a0 = a0 / pltpu.roll(a0, HEAD_DIM, 1)
            a1 = a1 / pltpu.roll(a1, HEAD_DIM, 1)
        o_ref[:, j * SLAB:(j + 1) * SLAB] = jnp.where(low, a0, pltpu.roll(a1, HEAD_DIM, 1)).astype(o_ref.dtype)


def _moba_prompt_body(kmax_ref, stab_ref, q_ref, k_ref, v_ref, km_ref, o_ref, qa_ref, m_ref, acc_ref, *, tq):
    i = pl.program_id(1)
    rows = GROUP * tq
    low = lax.broadcasted_iota(jnp.int32, (tq, SLAB), 1) < HEAD_DIM
    n_flags = SLOPE_LANE0 - FLAG_LANE0
    eligible = lax.broadcasted_iota(jnp.int32, (n_flags, tq), 0) < i
    krow = lax.broadcasted_iota(jnp.int32, (MOBA_BLOCK, SLAB), 0)
    klane = lax.broadcasted_iota(jnp.int32, (MOBA_BLOCK, SLAB), 1)
    klow = klane < HEAD_DIM
    one = jnp.ones((MOBA_BLOCK, SLAB), _BF16)

    def key_aux(block_offset, c):
        f = jnp.where(klane == FLAG_LANE0 + c, NEG_INF, 0.0)
        f = jnp.where(jnp.logical_and(klane >= SLOPE_LANE0, klane < SLOPE_LANE0 + 3), block_offset, f)
        f = jnp.where(jnp.logical_and(klane >= SLOPE_LANE0 + 3, klane < SLOPE_LANE0 + 6),
                      krow.astype(_F32), f)
        return f.astype(_BF16)

    def kv_head(n, _):
        km = km_ref[n]
        for g in range(GROUP):
            h = n * GROUP + g
            qh = q_ref[h]
            gate_t = lax.dot_general(km, qh, (((1,), (1,)), ((), ())),
                                     precision=lax.Precision.HIGHEST, preferred_element_type=_F32)
            sel_t = _top3_select(gate_t, eligible, 0)
            notsel_t = jnp.where(eligible, 1.0 - sel_t, 0.0)
            notsel = jnp.concatenate([jnp.zeros((FLAG_LANE0, tq), _F32), notsel_t,
                                      jnp.zeros((GATE_LANES - FLAG_LANE0 - n_flags, tq), _F32)],
                                     axis=0).T
            q2 = qh * (SCALE * LOG2E)
            qa_ref[h] = jnp.where(low, q2, notsel + stab_ref[h][0:1, :]).astype(_BF16)
            qn = jnp.sqrt(jnp.sum(jnp.where(low, q2 * q2, 0.0), axis=1, keepdims=True))
            m_ref[h] = jnp.broadcast_to(qn, (tq, SLAB))
        return 0

    lax.fori_loop(0, N_KV_HEADS, kv_head, 0)

    off = pl.multiple_of(i * MOBA_BLOCK, MOBA_BLOCK)
    f_own = key_aux(0.0, i)
    qrow = lax.broadcasted_iota(jnp.int32, (rows, MOBA_BLOCK), 0) & (tq - 1)
    kcol = lax.broadcasted_iota(jnp.int32, (rows, MOBA_BLOCK), 1)
    causal = qrow >= kcol
    keep = []
    for n in range(N_KV_HEADS):
        hs = pl.ds(n * GROUP, GROUP)
        kk = jnp.where(klow, k_ref[n, pl.ds(off, MOBA_BLOCK), :], f_own)
        vv = jnp.where(klow, v_ref[n, pl.ds(off, MOBA_BLOCK), :], one)
        s = _nt_dot(qa_ref[hs].reshape(rows, SLAB), kk)
        s = jnp.where(causal, s, NEG_INF)
        m0 = jnp.max(s, axis=1, keepdims=True)
        p = jnp.exp2(s - m0)
        qn = m_ref[hs].reshape(rows, SLAB)[:, 0:1]
        m_ref[hs] = jnp.broadcast_to(m0, (rows, SLAB)).reshape(GROUP, tq, SLAB)
        acc_ref[hs] = jnp.dot(p.astype(_BF16), vv, preferred_element_type=_F32).reshape(GROUP, tq, SLAB)
        slope_min = 2.0 ** (-8.0 * (n * GROUP + GROUP) / N_HEADS) * LOG2E * 0.999
        reach = (qn * (kmax_ref[pl.program_id(0) * N_KV_HEADS + n] * NORM_SLACK) + MOBA_DEAD - m0)
        blocks = jnp.max(reach) * (1.0 / (MOBA_BLOCK * slope_min)) - 1.0 / MOBA_BLOCK
        k_n = jnp.int32(0)
        for d in range(n_flags):
            k_n = k_n + (blocks >= float(d)).astype(jnp.int32)
        keep.append(k_n if n == 0 else jnp.maximum(k_n, keep[-1]))

    def past(c, _, groups):
        offc = pl.multiple_of(c * MOBA_BLOCK, MOBA_BLOCK)
        f_c = key_aux(((c - i) * MOBA_BLOCK).astype(_F32), c)
        for n in groups:
            hs = pl.ds(n * GROUP, GROUP)
            kk = jnp.where(klow, k_ref[n, pl.ds(offc, MOBA_BLOCK), :], f_c)
            vv = jnp.where(klow, v_ref[n, pl.ds(offc, MOBA_BLOCK), :], one)
            s = _nt_dot(qa_ref[hs].reshape(rows, SLAB), kk)
            m_old = m_ref[hs].reshape(rows, SLAB)
            m_new = jnp.maximum(m_old, jnp.max(s, axis=1, keepdims=True))
            alpha = jnp.exp2(m_old - m_new)
            p = jnp.exp2(s - jnp.concatenate([m_new, m_new], axis=1))
            pv = jnp.dot(p.astype(_BF16), vv, preferred_element_type=_F32)
            m_ref[hs] = m_new.reshape(GROUP, tq, SLAB)
            acc_ref[hs] = (alpha * acc_ref[hs].reshape(rows, SLAB) + pv).reshape(GROUP, tq, SLAB)
        return 0

    start = [jnp.maximum(i - k_n, 0) for k_n in keep]
    for n in range(N_KV_HEADS - 1, -1, -1):
        stop = start[n - 1] if n > 0 else i
        lax.fori_loop(start[n], stop, functools.partial(past, groups=tuple(range(n, N_KV_HEADS))), 0)
    _pair_heads(acc_ref, o_ref, normalise=True)


def _moba_prompt(kmax, stab, q, ks, vs, kmtab, *, batch, seq):
    tq = TQ
    nq = seq // tq
    return pl.pallas_call(
        functools.partial(_moba_prompt_body, tq=tq),
        grid=(batch, nq),
        in_specs=[
            pl.BlockSpec(memory_space=pltpu.SMEM),
            pl.BlockSpec(stab.shape, lambda b, i: (0, 0, 0)),
            pl.BlockSpec((N_HEADS, tq, SLAB), lambda b, i: (0, b * nq + i, 0)),
            pl.BlockSpec((N_KV_HEADS, seq, SLAB), lambda b, i: (0, b, 0)),
            pl.BlockSpec((N_KV_HEADS, seq, SLAB), lambda b, i: (0, b, 0)),
            pl.BlockSpec((None,) + kmtab.shape[1:], lambda b, i: (b, 0, 0, 0)),
        ],
        out_specs=pl.BlockSpec((tq, D_MODEL), lambda b, i: (b * nq + i, 0)),
        out_shape=jax.ShapeDtypeStruct((batch * seq, D_MODEL), _BF16),
        scratch_shapes=[pltpu.VMEM((N_HEADS, tq, SLAB), _BF16),
                        pltpu.VMEM((N_HEADS, tq, SLAB), _F32),
                        pltpu.VMEM((N_HEADS, tq, SLAB), _F32)],
        compiler_params=pltpu.CompilerParams(dimension_semantics=("arbitrary", "arbitrary"),
                                             vmem_limit_bytes=VMEM_LIMIT),
        name="moba_prompt",
    )(kmax, stab, q, ks, vs, kmtab)


def _stick_prompt_body(q_ref, k_ref, v_ref, o_ref, qa_ref, acc_ref, carry_ref, *, tq):
    i = pl.program_id(1)
    rows = GROUP * tq
    low = lax.broadcasted_iota(jnp.int32, q_ref.shape, 2) < HEAD_DIM
    qa_ref[...] = jnp.where(low, q_ref[...] * (SCALE * LOG2E), 0.0).astype(_BF16)
    acc_ref[...] = jnp.zeros(acc_ref.shape, _F32)
    carry_ref[...] = jnp.zeros(carry_ref.shape, _F32)
    tri = _stick_tri()
    qrow = lax.broadcasted_iota(jnp.int32, (rows, STICK_KB), 0) & (tq - 1)
    kcol = lax.broadcasted_iota(jnp.int32, (rows, STICK_KB), 1)
    rel = qrow - kcol
    last_chunk = (i + 1) * (tq // STICK_KB) - 1

    def cond(state):
        c, alive = state
        return jnp.logical_and(c >= 0, alive > 0)

    def step(state):
        c, _ = state
        off = pl.multiple_of(c * STICK_KB, STICK_KB)
        causal = rel + (i * tq - c * STICK_KB) > 0
        top = jnp.float32(-jnp.inf)
        for n in range(N_KV_HEADS):
            hs = pl.ds(n * GROUP, GROUP)
            z = _nt_dot(qa_ref[hs].reshape(rows, SLAB), k_ref[n, pl.ds(off, STICK_KB), :])
            pv, carry = _stick_chunk(z, causal, carry_ref[hs].reshape(rows, STICK_KB), tri,
                                     vc=v_ref[n, pl.ds(off, STICK_KB), :])
            acc_ref[hs] += pv.reshape(GROUP, tq, SLAB)
            carry_ref[hs] = carry.reshape(GROUP, tq, STICK_KB)
            top = jnp.maximum(top, jnp.max(carry))
        return c - 1, (top > STICK_DEAD).astype(jnp.int32)

    lax.while_loop(cond, step, (last_chunk, jnp.int32(1)))
    _pair_heads(acc_ref, o_ref, normalise=False)


def _stick_prompt(q, ks, vs, *, batch, seq):
    tq = STICK_TQ
    nq = seq // tq
    return pl.pallas_call(
        functools.partial(_stick_prompt_body, tq=tq),
        grid=(batch, nq),
        in_specs=[
            pl.BlockSpec((N_HEADS, tq, SLAB), lambda b, i: (0, b * nq + i, 0)),
            pl.BlockSpec((N_KV_HEADS, seq, SLAB), lambda b, i: (0, b, 0)),
            pl.BlockSpec((N_KV_HEADS, seq, SLAB), lambda b, i: (0, b, 0)),
        ],
        out_specs=pl.BlockSpec((tq, D_MODEL), lambda b, i: (b * nq + i, 0)),
        out_shape=jax.ShapeDtypeStruct((batch * seq, D_MODEL), _BF16),
        scratch_shapes=[pltpu.VMEM((N_HEADS, tq, SLAB), _BF16),
                        pltpu.VMEM((N_HEADS, tq, SLAB), _F32),
                        pltpu.VMEM((N_HEADS, tq, STICK_KB), _F32)],
        compiler_params=pltpu.CompilerParams(dimension_semantics=("arbitrary", "arbitrary"),
                                             vmem_limit_bytes=VMEM_LIMIT),
        name="stick_prompt",
    )(q, ks, vs)


def _sample_rows(q_ref, scale):
    q = q_ref[0] * scale
    grp = _lane_group(q.shape)
    return jnp.concatenate([jnp.where(grp == n, q, 0.0) for n in range(N_KV_HEADS)], axis=0)


def _sample_out(acc):
    r = acc.shape[0] // N_KV_HEADS
    grp = _lane_group((r, KV_DIM))
    out = jnp.zeros((r, KV_DIM), _F32)
    for n in range(N_KV_HEADS):
        out = jnp.where(grp == n, acc[n * r:(n + 1) * r], out)
    return out


def _moba_sample_body(pt_ref, slope_ref, qpos_ref, q_ref, kown_ref, vown_ref, *rest, n_steps):
    kp = rest[:PAGES_PER_STEP]
    vp = rest[PAGES_PER_STEP:2 * PAGES_PER_STEP]
    o_ref, oall_ref, mtab_ref, ltab_ref, kmt_ref = rest[2 * PAGES_PER_STEP:]
    sp = pl.program_id(1)
    qm = _sample_rows(q_ref, SCALE)
    qb = qm.astype(_BF16)
    rows = qm.shape[0]
    slope = slope_ref[...]
    qpos = qpos_ref[...]
    gcol = lax.broadcasted_iota(jnp.int32, (rows, GATE_LANES), 1)
    kmcol = lax.broadcasted_iota(jnp.int32, (KV_DIM, GATE_LANES), 1)
    kcol = lax.broadcasted_iota(jnp.int32, (rows, MOBA_BLOCK), 1).astype(_F32)
    pages_per_block = MOBA_BLOCK // PAGE_SIZE
    blocks_per_step = PAGES_PER_STEP // pages_per_block

    @pl.when(sp == 0)
    def _():
        for tab in (mtab_ref, ltab_ref, kmt_ref):
            tab[...] = jnp.zeros(tab.shape, _F32)

    for jb in range(blocks_per_step):
        m = sp * blocks_per_step + jb
        kblk = jnp.concatenate([kp[jb * pages_per_block + u][...] for u in range(pages_per_block)], axis=1)
        vblk = jnp.concatenate([vp[jb * pages_per_block + u][...] for u in range(pages_per_block)], axis=1)
        ksum = jnp.sum(kblk, axis=1, keepdims=True)
        kmt_ref[...] = jnp.where(kmcol == m, ksum * (1.0 / MOBA_BLOCK), kmt_ref[...])
        kpos = kcol + (m * MOBA_BLOCK).astype(_F32)
        s = jnp.dot(qb, kblk.astype(_BF16), preferred_element_type=_F32) - slope * (qpos - kpos)
        mb = jnp.max(s, axis=1, keepdims=True)
        p = jnp.exp(s - mb)
        oall_ref[m] = _nt_dot(p.astype(_BF16), vblk.astype(_BF16))
        here = gcol == m
        mtab_ref[...] = jnp.where(here, mb, mtab_ref[...])
        ltab_ref[...] = jnp.where(here, jnp.sum(p, axis=1, keepdims=True), ltab_ref[...])

    @pl.when(sp == n_steps - 1)
    def _():
        n_blocks = n_steps * blocks_per_step
        ocol = lax.broadcasted_iota(jnp.int32, (rows, PAGE_SIZE), 1).astype(_F32)
        dist = (qpos - (n_blocks * MOBA_BLOCK)) - ocol
        s = _nt_dot(qb, kown_ref[0]) - slope * dist
        s = jnp.where(dist >= 0.0, s, NEG_INF)
        m_own = jnp.max(s, axis=1, keepdims=True)
        p = jnp.exp(s - m_own)
        l_own = jnp.sum(p, axis=1, keepdims=True)
        o_own = jnp.dot(p.astype(_BF16), vown_ref[0], preferred_element_type=_F32)

        gate = jnp.dot(qm, kmt_ref[...], precision=lax.Precision.HIGHEST, preferred_element_type=_F32)
        picked = _top3_select(gate, gcol < n_blocks, 1) > 0.0
        mtab = mtab_ref[...]
        m_all = jnp.maximum(m_own, jnp.max(jnp.where(picked, mtab, NEG_INF), axis=1, keepdims=True))
        w = jnp.where(picked, jnp.exp(mtab - m_all), 0.0)
        w_own = jnp.exp(m_own - m_all)
        l_all = w_own * l_own + jnp.sum(w * ltab_ref[...], axis=1, keepdims=True)

        def merge(mi, acc):
            wm = jnp.sum(jnp.where(gcol == mi, w, 0.0), axis=1, keepdims=True)
            return acc + wm * oall_ref[mi]

        acc = lax.fori_loop(0, n_blocks, merge, w_own * o_own)
        o_ref[0] = _sample_out(acc / l_all).astype(o_ref.dtype)


def _page_specs(layer, page_of):
    def spec(j):
        return pl.BlockSpec((None, None, KV_DIM, PAGE_SIZE),
                            lambda b, s, pt, *_: (layer, pt[b, page_of(s, j)], 0, 0))
    return [spec(j) for j in range(PAGES_PER_STEP)] * 2


def _moba_sample(page_table, slope_rows, qpos_rows, q_rows, k_own, v_own, cache_kt, cache_vt, *, layer):
    dec_b, n_pages = page_table.shape
    n_steps = n_pages // PAGES_PER_STEP
    rows = q_rows.shape[1] * N_KV_HEADS
    n_blocks = n_pages * PAGE_SIZE // MOBA_BLOCK
    grid_spec = pltpu.PrefetchScalarGridSpec(
        num_scalar_prefetch=1,
        grid=(dec_b, n_steps),
        in_specs=[
            pl.BlockSpec((rows, 1), lambda b, s, pt: (0, 0)),
            pl.BlockSpec((rows, 1), lambda b, s, pt: (0, 0)),
            pl.BlockSpec((1, q_rows.shape[1], KV_DIM), lambda b, s, pt: (b, 0, 0)),
            pl.BlockSpec((1, PAGE_SIZE, KV_DIM), lambda b, s, pt: (b, 0, 0)),
            pl.BlockSpec((1, PAGE_SIZE, KV_DIM), lambda b, s, pt: (b, 0, 0)),
        ] + _page_specs(layer, lambda s, j: s * PAGES_PER_STEP + j),
        out_specs=pl.BlockSpec((1, q_rows.shape[1], KV_DIM), lambda b, s, pt: (b, 0, 0)),
        scratch_shapes=[pltpu.VMEM((n_blocks, rows, KV_DIM), _F32),
                        pltpu.VMEM((rows, GATE_LANES), _F32),
                        pltpu.VMEM((rows, GATE_LANES), _F32),
                        pltpu.VMEM((KV_DIM, GATE_LANES), _F32)],
    )
    return pl.pallas_call(
        functools.partial(_moba_sample_body, n_steps=n_steps),
        grid_spec=grid_spec,
        out_shape=jax.ShapeDtypeStruct(q_rows.shape, _BF16),
        compiler_params=pltpu.CompilerParams(dimension_semantics=("arbitrary", "arbitrary"),
                                             vmem_limit_bytes=VMEM_LIMIT),
        name="moba_sample",
    )(page_table, slope_rows, qpos_rows, q_rows, k_own, v_own,
      *([cache_kt] * PAGES_PER_STEP), *([cache_vt] * PAGES_PER_STEP))


def _stick_pages(qb, kp, vp, tri, acc_ref, carry_ref, alive_ref):
    for j in range(PAGES_PER_STEP):
        @pl.when(alive_ref[0] > 0)
        def _():
            z = jnp.dot(qb, kp[j][...].astype(_BF16), preferred_element_type=_F32)
            pv, carry = _stick_chunk(z, None, carry_ref[...], tri, vc_nt=vp[j][...].astype(_BF16))
            acc_ref[...] += pv
            carry_ref[...] = carry
            alive_ref[0] = (jnp.max(carry) > STICK_DEAD).astype(jnp.int32)


def _stick_sample_first_body(pt_ref, tok_ref, q_ref, kown_ref, vown_ref, *rest):
    kp = rest[:PAGES_PER_STEP]
    vp = rest[PAGES_PER_STEP:2 * PAGES_PER_STEP]
    o_ref, acc_ref, carry_ref, alive_ref = rest[2 * PAGES_PER_STEP:]
    qb = _sample_rows(q_ref, SCALE * LOG2E).astype(_BF16)
    rows = qb.shape[0]
    tri = _stick_tri()
    kcol = lax.broadcasted_iota(jnp.int32, (rows, STICK_KB), 1).astype(_F32)
    causal = kcol < tok_ref[...]
    z = _nt_dot(qb, kown_ref[0])
    pv, carry = _stick_chunk(z, causal, jnp.zeros((rows, STICK_KB), _F32), tri, vc=vown_ref[0])
    acc_ref[0] = pv
    carry_ref[0] = carry
    alive_ref[0] = (jnp.max(carry) > STICK_DEAD).astype(jnp.int32)
    _stick_pages(qb, kp, vp, tri, acc_ref.at[0], carry_ref.at[0], alive_ref)
    o_ref[0] = _sample_out(acc_ref[0]).astype(o_ref.dtype)


def _stick_sample_rest_body(pt_ref, live_ref, q_ref, acc_in_ref, carry_in_ref, *rest, n_steps):
    kp = rest[:PAGES_PER_STEP]
    vp = rest[PAGES_PER_STEP:2 * PAGES_PER_STEP]
    o_ref, acc_ref, carry_ref, alive_ref = rest[2 * PAGES_PER_STEP:]
    sp = pl.program_id(1)
    qb = _sample_rows(q_ref, SCALE * LOG2E).astype(_BF16)

    @pl.when(sp == 0)
    def _():
        acc_ref[...] = acc_in_ref[0]
        carry_ref[...] = carry_in_ref[0]
        alive_ref[0] = live_ref[pl.program_id(0)]

    _stick_pages(qb, kp, vp, _stick_tri(), acc_ref, carry_ref, alive_ref)

    @pl.when(sp == n_steps - 1)
    def _():
        o_ref[0] = _sample_out(acc_ref[...]).astype(o_ref.dtype)


def _stick_sample(page_table, tok_rows, q_rows, k_own, v_own, cache_kt, cache_vt, *, layer):
    dec_b, n_pages = page_table.shape
    n_steps = n_pages // PAGES_PER_STEP
    qr = q_rows.shape[1]
    rows = qr * N_KV_HEADS
    per_b = lambda b, s, *_: (b, 0, 0)
    params = pltpu.CompilerParams(dimension_semantics=("arbitrary", "arbitrary"),
                                  vmem_limit_bytes=VMEM_LIMIT)
    pages = [cache_kt] * PAGES_PER_STEP + [cache_vt] * PAGES_PER_STEP
    o_rows, acc, carry = pl.pallas_call(
        _stick_sample_first_body,
        grid_spec=pltpu.PrefetchScalarGridSpec(
            num_scalar_prefetch=1,
            grid=(dec_b, 1),
            in_specs=[
                pl.BlockSpec((rows, 1), lambda b, s, *_: (0, 0)),
                pl.BlockSpec((1, qr, KV_DIM), per_b),
                pl.BlockSpec((1, PAGE_SIZE, KV_DIM), per_b),
                pl.BlockSpec((1, PAGE_SIZE, KV_DIM), per_b),
            ] + _page_specs(layer, lambda s, j: n_pages - 1 - j),
            out_specs=[pl.BlockSpec((1, qr, KV_DIM), per_b),
                       pl.BlockSpec((1, rows, KV_DIM), per_b),
                       pl.BlockSpec((1, rows, STICK_KB), per_b)],
            scratch_shapes=[pltpu.SMEM((1,), jnp.int32)],
        ),
        out_shape=[jax.ShapeDtypeStruct(q_rows.shape, _BF16),
                   jax.ShapeDtypeStruct((dec_b, rows, KV_DIM), _F32),
                   jax.ShapeDtypeStruct((dec_b, rows, STICK_KB), _F32)],
        compiler_params=params,
        name="stick_sample_first",
    )(page_table, tok_rows, q_rows, k_own, v_own, *pages)
    if n_steps == 1:
        return o_rows
    live = (jnp.max(carry, axis=(1, 2)) > STICK_DEAD).astype(jnp.int32)
    pt_rest = jnp.where(live[:, None] > 0, page_table, page_table[0, 0])
    return pl.pallas_call(
        functools.partial(_stick_sample_rest_body, n_steps=n_steps - 1),
        grid_spec=pltpu.PrefetchScalarGridSpec(
            num_scalar_prefetch=2,
            grid=(dec_b, n_steps - 1),
            in_specs=[
                pl.BlockSpec((1, qr, KV_DIM), per_b),
                pl.BlockSpec((1, rows, KV_DIM), per_b),
                pl.BlockSpec((1, rows, STICK_KB), per_b),
            ] + _page_specs(layer, lambda s, j: n_pages - 1 - ((s + 1) * PAGES_PER_STEP + j)),
            out_specs=pl.BlockSpec((1, qr, KV_DIM), per_b),
            scratch_shapes=[pltpu.VMEM((rows, KV_DIM), _F32),
                            pltpu.VMEM((rows, STICK_KB), _F32),
                            pltpu.SMEM((1,), jnp.int32)],
        ),
        out_shape=jax.ShapeDtypeStruct(q_rows.shape, _BF16),
        compiler_params=params,
        name="stick_sample_rest",
    )(pt_rest, live, q_rows, acc, carry, *pages)


def _ffn_body(*refs, tm, tiles_per_seq, sample, final):
    it = iter(refs)
    x_ref, o_ref, wo_ref, g_ref, wup_ref, cw_ref, cb_ref, wdn_ref = (next(it) for _ in range(8))
    s1_ref = s2_ref = tok_ref = gfin_ref = None
    if sample:
        s1_ref, s2_ref, tok_ref = next(it), next(it), next(it)
    if final:
        gfin_ref = next(it)
    xo_ref, u_ref = next(it), next(it)
    y_ref = next(it) if final else None
    ubuf, carry, abuf = next(it), next(it), next(it)

    i = pl.program_id(0)
    x1 = x_ref[...] + jnp.dot(o_ref[...], wo_ref[...], preferred_element_type=_F32)
    ms = jnp.mean(x1 * x1, axis=-1, keepdims=True)
    h = ((x1 * lax.rsqrt(ms + RMS_EPS)) * g_ref[...]).astype(_BF16)
    xo_ref[...] = x1

    if not sample:
        @pl.when(i % tiles_per_seq == 0)
        def _():
            carry[...] = jnp.zeros(carry.shape, _F32)
    ubuf[:, 0:8, :] = jnp.zeros((ubuf.shape[0], 8, FF_CHUNK), _F32)

    def conv(jj, slot):
        u = jnp.dot(h, wup_ref[jj], preferred_element_type=_F32)
        buf = ubuf.at[slot]
        buf[8:8 + tm, :] = u
        if sample:
            u_ref[jj] = u
            tok = tok_ref[...]
            u1 = jnp.where(tok < 1.0, s1_ref[jj], buf[7:7 + tm, :])
            u2 = jnp.where(tok < 2.0, s2_ref[jj], buf[6:6 + tm, :])
        else:
            buf[6:8, :] = carry[jj, 6:8, :]
            u1 = buf[7:7 + tm, :]
            u2 = buf[6:6 + tm, :]
            carry[jj, 6:8, :] = u[tm - 2:tm, :]
        w = cw_ref[jj]
        return cb_ref[jj] + w[0:1, :] * u2 + w[1:2, :] * u1 + w[2:3, :] * u

    for j in range(N_FF_CHUNKS):
        cg = conv(j, 2 * (j % 2))
        cv = conv(j + N_FF_CHUNKS, 2 * (j % 2) + 1)
        act = (cg / (1.0 + jnp.exp(-cg))) * cv
        abuf[:, j * FF_CHUNK:(j + 1) * FF_CHUNK] = act.astype(_BF16)
    xo_ref[...] += jnp.dot(abuf[...], wdn_ref[...], preferred_element_type=_F32)
    if not sample:
        u_ref[...] = carry[...]
    if final:
        xo = xo_ref[...]
        ms2 = jnp.mean(xo * xo, axis=-1, keepdims=True)
        y_ref[...] = (xo * lax.rsqrt(ms2 + RMS_EPS)) * gfin_ref[...]


def _oproj_ffn(x, o, wo, g, wup, cw, cb, wdn, *, tm, seq, sample_state=None, g_final=None):
    t = x.shape[0]
    sample = sample_state is not None
    final = g_final is not None
    n_tiles = t // tm
    tiles_per_seq = max(seq // tm, 1)
    const2 = lambda i: (0, 0)
    const3 = lambda i: (0, 0, 0)
    once = pl.Buffered(1)
    in_specs = [
        pl.BlockSpec((tm, D_MODEL), lambda i: (i, 0)),
        pl.BlockSpec((tm, D_MODEL), lambda i: (i, 0)),
        pl.BlockSpec(wo.shape, const2, pipeline_mode=once),
        pl.BlockSpec((1, D_MODEL), const2),
        pl.BlockSpec(wup.shape, const3, pipeline_mode=once),
        pl.BlockSpec(cw.shape, const3),
        pl.BlockSpec(cb.shape, const3),
        pl.BlockSpec(wdn.shape, const2, pipeline_mode=once),
    ]
    args = [x, o, wo, g, wup, cw, cb, wdn]
    if sample:
        s1, s2, tok = sample_state
        in_specs += [pl.BlockSpec(s1.shape, const3), pl.BlockSpec(s2.shape, const3),
                     pl.BlockSpec(tok.shape, const2)]
        args += [s1, s2, tok]
    if final:
        in_specs.append(pl.BlockSpec((1, D_MODEL), const2))
        args.append(g_final)
    out_shape = [jax.ShapeDtypeStruct((t, D_MODEL), _F32)]
    out_specs = [pl.BlockSpec((tm, D_MODEL), lambda i: (i, 0))]
    if sample:
        out_shape.append(jax.ShapeDtypeStruct((2 * N_FF_CHUNKS, t, FF_CHUNK), _F32))
        out_specs.append(pl.BlockSpec((2 * N_FF_CHUNKS, tm, FF_CHUNK), lambda i: (0, i, 0)))
    else:
        n_seq = t // seq
        out_shape.append(jax.ShapeDtypeStruct((n_seq, 2 * N_FF_CHUNKS, 8, FF_CHUNK), _F32))
        out_specs.append(pl.BlockSpec((None, 2 * N_FF_CHUNKS, 8, FF_CHUNK),
                                      lambda i: (i // tiles_per_seq, 0, 0, 0)))
    if final:
        out_shape.append(jax.ShapeDtypeStruct((t, D_MODEL), _F32))
        out_specs.append(pl.BlockSpec((tm, D_MODEL), lambda i: (i, 0)))
    return pl.pallas_call(
        functools.partial(_ffn_body, tm=tm, tiles_per_seq=tiles_per_seq, sample=sample, final=final),
        grid=(n_tiles,),
        in_specs=in_specs,
        out_specs=out_specs,
        out_shape=out_shape,
        scratch_shapes=[pltpu.VMEM((4, tm + 8, FF_CHUNK), _F32),
                        pltpu.VMEM((2 * N_FF_CHUNKS, 8, FF_CHUNK), _F32),
                        pltpu.VMEM((tm, D_FF), _BF16)],
        compiler_params=pltpu.CompilerParams(dimension_semantics=("arbitrary",),
                                             vmem_limit_bytes=VMEM_LIMIT),
        name="oproj_ffn_sample" if sample else "oproj_ffn",
    )(*args)


def _chunk_cols(a):
    lead = a.shape[:-2]
    r = a.shape[-2]
    a = a.reshape(lead + (r, 2 * N_FF_CHUNKS, FF_CHUNK))
    return jnp.moveaxis(a, -2, -3)


def _unchunk_cols(a):
    a = jnp.moveaxis(a, -3, -2)
    return a.reshape(a.shape[:-2] + (2 * D_FF,))


def _swap_halves(w):
    s = w.shape
    return w.reshape(s[:-1] + (s[-1] // SLAB, 2, HEAD_DIM))[..., ::-1, :].reshape(s)


def _bf16_pieces(x):
    hi = x.astype(_BF16).astype(_F32)
    mid = (x - hi).astype(_BF16).astype(_F32)
    lo = (x - hi - mid).astype(_BF16).astype(_F32)
    return hi, mid, lo


def kernel(x_prompt, x_sample, cache_k, cache_v, state_conv, page_table, g_attn, w_qkv, w_o,
           g_ffn, w_up, conv_w, conv_b, w_down, g_final):
    depth = w_qkv.shape[0]
    batch, seq, _ = x_prompt.shape
    dec_b, dec_s, _ = x_sample.shape
    n_pool = cache_k.shape[1]
    n_pages = page_table.shape[1]
    past = n_pages * PAGE_SIZE
    tp = batch * seq
    ts = dec_b * dec_s
    nb = seq // MOBA_BLOCK

    wq, wk, wv = (w_qkv[:, :, :D_MODEL], w_qkv[:, :, D_MODEL:D_MODEL + KV_DIM],
                  w_qkv[:, :, D_MODEL + KV_DIM:])
    wqkv = jnp.concatenate([wq, _swap_halves(wq), wk, wv, _swap_halves(wk), _swap_halves(wv)],
                           axis=2).astype(_BF16)
    wo = w_o.astype(_BF16)
    wup = _chunk_cols(w_up).astype(_BF16)
    cw = _chunk_cols(conv_w)
    cb = _chunk_cols(conv_b[:, None, :])
    wdn = w_down.astype(_BF16)
    g_attn2 = g_attn[:, None, :]
    g_ffn2 = g_ffn[:, None, :]
    g_fin2 = g_final[None, :]

    hidx = jnp.arange(1, N_HEADS + 1, dtype=_F32)
    slopes = jnp.exp2(-8.0 * hidx / N_HEADS)
    assert nb <= SLOPE_LANE0 - FLAG_LANE0 and past % MOBA_BLOCK == 0 and dec_s <= PAGE_SIZE
    pieces = jnp.stack(_bf16_pieces(slopes * LOG2E) * 2, axis=1)
    stab = jnp.zeros((N_HEADS, 8, SLAB), _F32).at[:, :, SLOPE_LANE0:SLOPE_LANE0 + 6].set(pieces[:, None, :])
    slope_rows = jnp.repeat(slopes, dec_s)[:, None]
    tok_rows = jnp.tile(jnp.arange(dec_s, dtype=_F32), N_HEADS)[:, None]
    qpos_rows = tok_rows + float(past)
    tok_seq = jnp.tile(jnp.arange(dec_s, dtype=_F32), dec_b)[:, None]

    cache_kt = cache_k.transpose(0, 1, 3, 4, 2).reshape(depth, n_pool, KV_DIM, PAGE_SIZE)
    cache_vt = cache_v.transpose(0, 1, 3, 4, 2).reshape(depth, n_pool, KV_DIM, PAGE_SIZE)

    st = state_conv
    zero = jnp.zeros_like(st[:, :, :1])
    s1 = jnp.concatenate([st[:, :, 1:2], zero, zero, zero][:dec_s], axis=2)
    s2 = jnp.concatenate([st[:, :, 0:1], st[:, :, 1:2], zero, zero][:dec_s], axis=2)
    s1 = _chunk_cols(s1.reshape(depth, ts, 2 * D_FF))
    s2 = _chunk_cols(s2.reshape(depth, ts, 2 * D_FF))

    xp = x_prompt.reshape(tp, D_MODEL)
    xs = x_sample.reshape(ts, D_MODEL)
    kp_l, vp_l, ks_l, vs_l, cp_l, cs_l = [], [], [], [], [], []
    yp = ys = None
    pad_own = jnp.zeros((dec_b, PAGE_SIZE - dec_s, KV_DIM), _BF16)
    for l in range(depth):
        last = l == depth - 1
        gfin = g_fin2 if last else None
        moba = l % 2 == 0
        q, k, v, kslab, vslab, *km = _rms_qkv(xp, g_attn2[l], wqkv[l], tm=512, with_kmean=moba)
        if moba:
            kmt = km[0].reshape(batch, nb, N_KV_HEADS, HEAD_DIM).transpose(0, 2, 1, 3)
            kmt = jnp.pad(kmt, ((0, 0), (0, 0), (0, SLOPE_LANE0 - FLAG_LANE0 - nb),
                                (0, SLAB - HEAD_DIM)))
            knorm = jnp.sqrt(jnp.max(km[1].reshape(batch, nb, N_KV_HEADS, HEAD_DIM), axis=(1, 3)))
            o = _moba_prompt(knorm.reshape(batch * N_KV_HEADS), stab, q, kslab, vslab, kmt,
                             batch=batch, seq=seq)
        else:
            o = _stick_prompt(q, kslab, vslab, batch=batch, seq=seq)
        outs = _oproj_ffn(xp, o, wo[l], g_ffn2[l], wup[l], cw[l], cb[l], wdn[l],
                          tm=512, seq=seq, g_final=gfin)
        xp, cst = outs[0], outs[1]
        if last:
            yp = outs[2]
        kp_l.append(k.reshape(batch, seq, N_KV_HEADS, HEAD_DIM))
        vp_l.append(v.reshape(batch, seq, N_KV_HEADS, HEAD_DIM))
        cp_l.append(_unchunk_cols(cst[:, :, 6:8, :]))
        qs, ksn, vsn, _, _ = _rms_qkv(xs, g_attn2[l], wqkv[l], tm=ts, with_kmean=False)
        qnat = qs[:, :, :HEAD_DIM].reshape(N_KV_HEADS, GROUP, dec_b, dec_s, HEAD_DIM)
        q_rows = qnat.transpose(2, 1, 3, 0, 4).reshape(dec_b, GROUP * dec_s, KV_DIM)
        k_own = jnp.concatenate([ksn.astype(_BF16).reshape(dec_b, dec_s, KV_DIM), pad_own], axis=1)
        v_own = jnp.concatenate([vsn.astype(_BF16).reshape(dec_b, dec_s, KV_DIM), pad_own], axis=1)
        if moba:
            o_rows = _moba_sample(page_table, slope_rows, qpos_rows, q_rows, k_own, v_own,
                                  cache_kt, cache_vt, layer=l)
        else:
            o_rows = _stick_sample(page_table, tok_rows, q_rows, k_own, v_own, cache_kt, cache_vt, layer=l)
        os_ = o_rows.reshape(dec_b, GROUP, dec_s, N_KV_HEADS, HEAD_DIM).transpose(0, 2, 3, 1, 4)
        outs = _oproj_ffn(xs, os_.reshape(ts, D_MODEL), wo[l], g_ffn2[l], wup[l], cw[l], cb[l], wdn[l],
                          tm=ts, seq=dec_s, sample_state=(s1[l], s2[l], tok_seq), g_final=gfin)
        xs, u_all = outs[0], outs[1]
        if last:
            ys = outs[2]
        ks_l.append(ksn.reshape(dec_b, dec_s, N_KV_HEADS, HEAD_DIM))
        vs_l.append(vsn.reshape(dec_b, dec_s, N_KV_HEADS, HEAD_DIM))
        u_full = _unchunk_cols(u_all).reshape(dec_b, dec_s, 2 * D_FF)
        cs_l.append(u_full[:, dec_s - (CONV_W - 1):, :])
    return (yp.reshape(batch, seq, D_MODEL), ys.reshape(dec_b, dec_s, D_MODEL),
            jnp.stack(kp_l), jnp.stack(vp_l), jnp.stack(ks_l), jnp.stack(vs_l),
            jnp.stack(cp_l), jnp.stack(cs_l))
```

```python
import functools

import jax
import jax.numpy as jnp
from jax import lax
from jax.experimental import pallas as pl
from jax.experimental.pallas import tpu as pltpu

D_MODEL = 1024
N_HEADS = 16
HEAD_DIM = 64
N_KV_HEADS = 4
GROUP = N_HEADS // N_KV_HEADS
KV_DIM = N_KV_HEADS * HEAD_DIM
SLAB = 2 * HEAD_DIM
MOBA_BLOCK = 256
MOBA_TOPK = 3
D_FF = 2816
CONV_W = 3
PAGE_SIZE = 128
RMS_EPS = 1e-6
NEG_INF = -1e30
SCALE = HEAD_DIM ** -0.5

FF_CHUNK = 256
N_FF_CHUNKS = D_FF // FF_CHUNK
STICK_KB = 128
LOG2E = 1.4426950408889634
STICK_DEAD = -110.0 * LOG2E
MOBA_DEAD = 152.0
NORM_SLACK = 1.04
GATE_LANES = 128
FLAG_LANE0 = HEAD_DIM
SLOPE_LANE0 = 96
PAGES_PER_STEP = 8
MOBA_PAGES_PER_STEP = 16
VMEM_LIMIT = 56 * 1024 * 1024
TQ = 256
STICK_TQ = 256

_BF16 = jnp.bfloat16
_F32 = jnp.float32


def _nt_dot(a, b):
    return lax.dot_general(a, b, (((1,), (1,)), ((), ())), preferred_element_type=_F32)


def _lane_group(shape):
    return lax.broadcasted_iota(jnp.int32, shape, len(shape) - 1) >> 6


def _qkv_body(x_ref, g_ref, w_ref, q_ref, k_ref, v_ref, ks_ref, vs_ref, *km_ref, tm):
    x = x_ref[...]
    ms = jnp.mean(x * x, axis=-1, keepdims=True)
    h = (x * lax.rsqrt(ms + RMS_EPS)) * g_ref[...]
    qkv = jnp.dot(h.astype(_BF16), w_ref[...], preferred_element_type=_F32)
    for j in range(N_HEADS // 2):
        q_ref[2 * j] = qkv[:, j * SLAB:(j + 1) * SLAB]
        q_ref[2 * j + 1] = qkv[:, D_MODEL + j * SLAB:D_MODEL + (j + 1) * SLAB]
    base = 2 * D_MODEL
    k = qkv[:, base:base + KV_DIM]
    v = qkv[:, base + KV_DIM:base + 2 * KV_DIM]
    k_sw = qkv[:, base + 2 * KV_DIM:base + 3 * KV_DIM]
    v_sw = qkv[:, base + 3 * KV_DIM:base + 4 * KV_DIM]
    k_ref[...] = k
    v_ref[...] = v
    for j in range(N_KV_HEADS // 2):
        ks_ref[2 * j] = k[:, j * SLAB:(j + 1) * SLAB].astype(_BF16)
        ks_ref[2 * j + 1] = k_sw[:, j * SLAB:(j + 1) * SLAB].astype(_BF16)
        vs_ref[2 * j] = v[:, j * SLAB:(j + 1) * SLAB].astype(_BF16)
        vs_ref[2 * j + 1] = v_sw[:, j * SLAB:(j + 1) * SLAB].astype(_BF16)
    if km_ref:
        li = lax.broadcasted_iota(jnp.int32, (KV_DIM, KV_DIM), 0) >> 6
        lj = lax.broadcasted_iota(jnp.int32, (KV_DIM, KV_DIM), 1) >> 6
        k2 = jnp.dot((k * k).astype(_BF16), jnp.where(li == lj, 1.0, 0.0).astype(_BF16),
                     preferred_element_type=_F32)
        for j in range(tm // MOBA_BLOCK):
            blk = slice(j * MOBA_BLOCK, (j + 1) * MOBA_BLOCK)
            km_ref[0][j] = jnp.mean(k[blk], axis=0, keepdims=True)
            km_ref[1][j] = jnp.max(k2[blk], axis=0, keepdims=True)


def _rms_qkv(x, g, w, *, tm, with_kmean):
    t = x.shape[0]
    out_shape = [
        jax.ShapeDtypeStruct((N_HEADS, t, SLAB), _F32),
        jax.ShapeDtypeStruct((t, KV_DIM), _F32),
        jax.ShapeDtypeStruct((t, KV_DIM), _F32),
        jax.ShapeDtypeStruct((N_KV_HEADS, t, SLAB), _BF16),
        jax.ShapeDtypeStruct((N_KV_HEADS, t, SLAB), _BF16),
    ]
    row = pl.BlockSpec((tm, KV_DIM), lambda i: (i, 0))
    slab = pl.BlockSpec((N_KV_HEADS, tm, SLAB), lambda i: (0, i, 0))
    out_specs = [pl.BlockSpec((N_HEADS, tm, SLAB), lambda i: (0, i, 0)), row, row, slab, slab]
    if with_kmean:
        for _ in range(2):
            out_shape.append(jax.ShapeDtypeStruct((t // MOBA_BLOCK, 1, KV_DIM), _F32))
            out_specs.append(pl.BlockSpec((tm // MOBA_BLOCK, 1, KV_DIM), lambda i: (i, 0, 0)))
    return pl.pallas_call(
        functools.partial(_qkv_body, tm=tm),
        grid=(t // tm,),
        in_specs=[
            pl.BlockSpec((tm, D_MODEL), lambda i: (i, 0)),
            pl.BlockSpec((1, D_MODEL), lambda i: (0, 0)),
            pl.BlockSpec(w.shape, lambda i: (0, 0)),
        ],
        out_specs=out_specs,
        out_shape=out_shape,
        compiler_params=pltpu.CompilerParams(dimension_semantics=("arbitrary",),
                                             vmem_limit_bytes=VMEM_LIMIT),
        name="rms_qkv",
    )(x, g, w)


def _top3_select(gate, eligible, axis):
    idx = lax.broadcasted_iota(jnp.int32, gate.shape, axis).astype(_F32)
    neg = jnp.float32(-jnp.inf)
    gm = jnp.where(eligible, gate, neg)
    sel = jnp.zeros(gate.shape, _F32)
    for _ in range(MOBA_TOPK):
        mx = jnp.max(gm, axis=axis, keepdims=True)
        first = jnp.min(jnp.where(gm == mx, idx, jnp.float32(1e9)), axis=axis, keepdims=True)
        pick = jnp.logical_and(idx == first, mx > neg)
        sel = jnp.where(pick, 1.0, sel)
        gm = jnp.where(pick, neg, gm)
    return sel


def _stick_tri():
    j = lax.broadcasted_iota(jnp.int32, (2 * STICK_KB, 2 * STICK_KB), 0) & (STICK_KB - 1)
    s = lax.broadcasted_iota(jnp.int32, (2 * STICK_KB, 2 * STICK_KB), 1)
    return jnp.where(jnp.logical_or(s >= STICK_KB, j > s), 1.0, 0.0).astype(_BF16)


def _stick_chunk(z, causal, carry, tri, vc_nt=None, vc=None):
    nz = -z
    e = jnp.exp2(jnp.minimum(z, nz))
    log_keep = jnp.minimum(nz, 0.0) - jnp.log2(1.0 + e)
    log_sig = z + log_keep
    if causal is not None:
        log_keep = jnp.where(causal, log_keep, 0.0)
    hi = log_keep.astype(_BF16)
    lo = (log_keep - hi.astype(_F32)).astype(_BF16)
    ct = jnp.dot(jnp.concatenate([hi, lo], axis=1), tri, preferred_element_type=_F32)
    log_after = carry + ct[:, :STICK_KB]
    a = jnp.exp2(log_sig + log_after)
    if causal is not None:
        a = jnp.where(causal, a, 0.0)
    a = a.astype(_BF16)
    pv = _nt_dot(a, vc_nt) if vc is None else jnp.dot(a, vc, preferred_element_type=_F32)
    return pv, carry + ct[:, STICK_KB:]


def _pair_heads(acc_ref, o_ref, normalise):
    low = lax.broadcasted_iota(jnp.int32, acc_ref.shape[1:], 1) < HEAD_DIM
    for j in range(N_HEADS // 2):
        a0 = acc_ref[2 * j]
        a1 = acc_ref[2 * j + 1]
        if normalise:
            a0 = a0 / pltpu.roll(a0, HEAD_DIM, 1)
            a1 = a1 / pltpu.roll(a1, HEAD_DIM, 1)
        o_ref[:, j * SLAB:(j + 1) * SLAB] = jnp.where(low, a0, pltpu.roll(a1, HEAD_DIM, 1)).astype(o_ref.dtype)


def _moba_prompt_body(kmax_ref, stab_ref, q_ref, k_ref, v_ref, km_ref, o_ref, qa_ref, m_ref, acc_ref, *, tq):
    i = pl.program_id(1)
    rows = GROUP * tq
    low = lax.broadcasted_iota(jnp.int32, (tq, SLAB), 1) < HEAD_DIM
    n_flags = SLOPE_LANE0 - FLAG_LANE0
    eligible = lax.broadcasted_iota(jnp.int32, (n_flags, tq), 0) < i
    krow = lax.broadcasted_iota(jnp.int32, (MOBA_BLOCK, SLAB), 0)
    klane = lax.broadcasted_iota(jnp.int32, (MOBA_BLOCK, SLAB), 1)
    klow = klane < HEAD_DIM
    one = jnp.ones((MOBA_BLOCK, SLAB), _BF16)

    def key_aux(block_offset, c):
        f = jnp.where(klane == FLAG_LANE0 + c, NEG_INF, 0.0)
        f = jnp.where(jnp.logical_and(klane >= SLOPE_LANE0, klane < SLOPE_LANE0 + 3), block_offset, f)
        f = jnp.where(jnp.logical_and(klane >= SLOPE_LANE0 + 3, klane < SLOPE_LANE0 + 6),
                      krow.astype(_F32), f)
        return f.astype(_BF16)

    def kv_head(n, _):
        km = km_ref[n]
        for g in range(GROUP):
            h = n * GROUP + g
            qh = q_ref[h]
            gate_t = lax.dot_general(km, qh, (((1,), (1,)), ((), ())),
                                     precision=lax.Precision.HIGHEST, preferred_element_type=_F32)
            sel_t = _top3_select(gate_t, eligible, 0)
            notsel_t = jnp.where(eligible, 1.0 - sel_t, 0.0)
            notsel = jnp.concatenate([jnp.zeros((FLAG_LANE0, tq), _F32), notsel_t,
                                      jnp.zeros((GATE_LANES - FLAG_LANE0 - n_flags, tq), _F32)],
                                     axis=0).T
            q2 = qh * (SCALE * LOG2E)
            qa_ref[h] = jnp.where(low, q2, notsel + stab_ref[h][0:1, :]).astype(_BF16)
            qn = jnp.sqrt(jnp.sum(jnp.where(low, q2 * q2, 0.0), axis=1, keepdims=True))
            m_ref[h] = jnp.broadcast_to(qn, (tq, SLAB))
        return 0

    lax.fori_loop(0, N_KV_HEADS, kv_head, 0)

    off = pl.multiple_of(i * MOBA_BLOCK, MOBA_BLOCK)
    f_own = key_aux(0.0, i)
    qrow = lax.broadcasted_iota(jnp.int32, (rows, MOBA_BLOCK), 0) & (tq - 1)
    kcol = lax.broadcasted_iota(jnp.int32, (rows, MOBA_BLOCK), 1)
    causal = qrow >= kcol
    keep = []
    for n in range(N_KV_HEADS):
        hs = pl.ds(n * GROUP, GROUP)
        kk = jnp.where(klow, k_ref[n, pl.ds(off, MOBA_BLOCK), :], f_own)
        vv = jnp.where(klow, v_ref[n, pl.ds(off, MOBA_BLOCK), :], one)
        s = _nt_dot(qa_ref[hs].reshape(rows, SLAB), kk)
        s = jnp.where(causal, s, NEG_INF)
        m0 = jnp.max(s, axis=1, keepdims=True)
        p = jnp.exp2(s - m0)
        qn = m_ref[hs].reshape(rows, SLAB)[:, 0:1]
        m_ref[hs] = jnp.broadcast_to(m0, (rows, SLAB)).reshape(GROUP, tq, SLAB)
        acc_ref[hs] = jnp.dot(p.astype(_BF16), vv, preferred_element_type=_F32).reshape(GROUP, tq, SLAB)
        slope_min = 2.0 ** (-8.0 * (n * GROUP + GROUP) / N_HEADS) * LOG2E * 0.999
        reach = (qn * (kmax_ref[pl.program_id(0) * N_KV_HEADS + n] * NORM_SLACK) + MOBA_DEAD - m0)
        blocks = jnp.max(reach) * (1.0 / (MOBA_BLOCK * slope_min)) - 1.0 / MOBA_BLOCK
        k_n = jnp.int32(0)
        for d in range(n_flags):
            k_n = k_n + (blocks >= float(d)).astype(jnp.int32)
        keep.append(k_n if n == 0 else jnp.maximum(k_n, keep[-1]))

    def past(c, _, groups):
        offc = pl.multiple_of(c * MOBA_BLOCK, MOBA_BLOCK)
        f_c = key_aux(((c - i) * MOBA_BLOCK).astype(_F32), c)
        for n in groups:
            hs = pl.ds(n * GROUP, GROUP)
            kk = jnp.where(klow, k_ref[n, pl.ds(offc, MOBA_BLOCK), :], f_c)
            vv = jnp.where(klow, v_ref[n, pl.ds(offc, MOBA_BLOCK), :], one)
            s = _nt_dot(qa_ref[hs].reshape(rows, SLAB), kk)
            m_old = m_ref[hs].reshape(rows, SLAB)
            m_new = jnp.maximum(m_old, jnp.max(s, axis=1, keepdims=True))
            alpha = jnp.exp2(m_old - m_new)
            p = jnp.exp2(s - jnp.concatenate([m_new, m_new], axis=1))
            pv = jnp.dot(p.astype(_BF16), vv, preferred_element_type=_F32)
            m_ref[hs] = m_new.reshape(GROUP, tq, SLAB)
            acc_ref[hs] = (alpha * acc_ref[hs].reshape(rows, SLAB) + pv).reshape(GROUP, tq, SLAB)
        return 0

    start = [jnp.maximum(i - k_n, 0) for k_n in keep]
    for n in range(N_KV_HEADS - 1, -1, -1):
        stop = start[n - 1] if n > 0 else i
        lax.fori_loop(start[n], stop, functools.partial(past, groups=tuple(range(n, N_KV_HEADS))), 0)
    _pair_heads(acc_ref, o_ref, normalise=True)


def _moba_prompt(kmax, stab, q, ks, vs, kmtab, *, batch, seq):
    tq = TQ
    nq = seq // tq
    return pl.pallas_call(
        functools.partial(_moba_prompt_body, tq=tq),
        grid=(batch, nq),
        in_specs=[
            pl.BlockSpec(memory_space=pltpu.SMEM),
            pl.BlockSpec(stab.shape, lambda b, i: (0, 0, 0)),
            pl.BlockSpec((N_HEADS, tq, SLAB), lambda b, i: (0, b * nq + i, 0)),
            pl.BlockSpec((N_KV_HEADS, seq, SLAB), lambda b, i: (0, b, 0)),
            pl.BlockSpec((N_KV_HEADS, seq, SLAB), lambda b, i: (0, b, 0)),
            pl.BlockSpec((None,) + kmtab.shape[1:], lambda b, i: (b, 0, 0, 0)),
        ],
        out_specs=pl.BlockSpec((tq, D_MODEL), lambda b, i: (b * nq + i, 0)),
        out_shape=jax.ShapeDtypeStruct((batch * seq, D_MODEL), _BF16),
        scratch_shapes=[pltpu.VMEM((N_HEADS, tq, SLAB), _BF16),
                        pltpu.VMEM((N_HEADS, tq, SLAB), _F32),
                        pltpu.VMEM((N_HEADS, tq, SLAB), _F32)],
        compiler_params=pltpu.CompilerParams(dimension_semantics=("arbitrary", "arbitrary"),
                                             vmem_limit_bytes=VMEM_LIMIT),
        name="moba_prompt",
    )(kmax, stab, q, ks, vs, kmtab)


def _stick_prompt_body(q_ref, k_ref, v_ref, o_ref, qa_ref, acc_ref, carry_ref, *, tq):
    i = pl.program_id(1)
    rows = GROUP * tq
    low = lax.broadcasted_iota(jnp.int32, q_ref.shape, 2) < HEAD_DIM
    qa_ref[...] = jnp.where(low, q_ref[...] * (SCALE * LOG2E), 0.0).astype(_BF16)
    acc_ref[...] = jnp.zeros(acc_ref.shape, _F32)
    carry_ref[...] = jnp.zeros(carry_ref.shape, _F32)
    tri = _stick_tri()
    qrow = lax.broadcasted_iota(jnp.int32, (rows, STICK_KB), 0) & (tq - 1)
    kcol = lax.broadcasted_iota(jnp.int32, (rows, STICK_KB), 1)
    rel = qrow - kcol
    chunks_per_tile = tq // STICK_KB

    def chunk(c, masked):
        off = pl.multiple_of(c * STICK_KB, STICK_KB)
        causal = (rel + (i * tq - c * STICK_KB) > 0) if masked else None
        top = jnp.float32(-jnp.inf)
        for n in range(N_KV_HEADS):
            hs = pl.ds(n * GROUP, GROUP)
            z = _nt_dot(qa_ref[hs].reshape(rows, SLAB), k_ref[n, pl.ds(off, STICK_KB), :])
            pv, carry = _stick_chunk(z, causal, carry_ref[hs].reshape(rows, STICK_KB), tri,
                                     vc=v_ref[n, pl.ds(off, STICK_KB), :])
            acc_ref[hs] += pv.reshape(GROUP, tq, SLAB)
            carry_ref[hs] = carry.reshape(GROUP, tq, STICK_KB)
            top = jnp.maximum(top, jnp.max(carry))
        return top

    top = jnp.float32(0.0)
    for d in range(chunks_per_tile):
        top = chunk((i + 1) * chunks_per_tile - 1 - d, True)

    def cond(state):
        c, alive = state
        return jnp.logical_and(c >= 0, alive > 0)

    def step(state):
        c, _ = state
        return c - 1, (chunk(c, False) > STICK_DEAD).astype(jnp.int32)

    lax.while_loop(cond, step, (i * chunks_per_tile - 1, (top > STICK_DEAD).astype(jnp.int32)))
    _pair_heads(acc_ref, o_ref, normalise=False)


def _stick_prompt(q, ks, vs, *, batch, seq):
    tq = STICK_TQ
    nq = seq // tq
    return pl.pallas_call(
        functools.partial(_stick_prompt_body, tq=tq),
        grid=(batch, nq),
        in_specs=[
            pl.BlockSpec((N_HEADS, tq, SLAB), lambda b, i: (0, b * nq + i, 0)),
            pl.BlockSpec((N_KV_HEADS, seq, SLAB), lambda b, i: (0, b, 0)),
            pl.BlockSpec((N_KV_HEADS, seq, SLAB), lambda b, i: (0, b, 0)),
        ],
        out_specs=pl.BlockSpec((tq, D_MODEL), lambda b, i: (b * nq + i, 0)),
        out_shape=jax.ShapeDtypeStruct((batch * seq, D_MODEL), _BF16),
        scratch_shapes=[pltpu.VMEM((N_HEADS, tq, SLAB), _BF16),
                        pltpu.VMEM((N_HEADS, tq, SLAB), _F32),
                        pltpu.VMEM((N_HEADS, tq, STICK_KB), _F32)],
        compiler_params=pltpu.CompilerParams(dimension_semantics=("arbitrary", "arbitrary"),
                                             vmem_limit_bytes=VMEM_LIMIT),
        name="stick_prompt",
    )(q, ks, vs)


def _sample_rows(q_ref, scale):
    q = q_ref[0] * scale
    grp = _lane_group(q.shape)
    return jnp.concatenate([jnp.where(grp == n, q, 0.0) for n in range(N_KV_HEADS)], axis=0)


def _sample_out(acc):
    r = acc.shape[0] // N_KV_HEADS
    grp = _lane_group((r, KV_DIM))
    out = jnp.zeros((r, KV_DIM), _F32)
    for n in range(N_KV_HEADS):
        out = jnp.where(grp == n, acc[n * r:(n + 1) * r], out)
    return out


def _moba_sample_body(pt_ref, slope_ref, qpos_ref, q_ref, kown_ref, vown_ref, *rest, n_steps, pps):
    kp = rest[:pps]
    vp = rest[pps:2 * pps]
    o_ref, oall_ref, mtab_ref, ltab_ref, kmt_ref = rest[2 * pps:]
    sp = pl.program_id(1)
    qm = _sample_rows(q_ref, SCALE)
    qb = qm.astype(_BF16)
    rows = qm.shape[0]
    slope = slope_ref[...]
    qpos = qpos_ref[...]
    gcol = lax.broadcasted_iota(jnp.int32, (rows, GATE_LANES), 1)
    kmcol = lax.broadcasted_iota(jnp.int32, (KV_DIM, GATE_LANES), 1)
    kcol = lax.broadcasted_iota(jnp.int32, (rows, MOBA_BLOCK), 1).astype(_F32)
    pages_per_block = MOBA_BLOCK // PAGE_SIZE
    blocks_per_step = pps // pages_per_block

    @pl.when(sp == 0)
    def _():
        for tab in (mtab_ref, ltab_ref, kmt_ref):
            tab[...] = jnp.zeros(tab.shape, _F32)

    kmt, mtab, ltab = kmt_ref[...], mtab_ref[...], ltab_ref[...]
    for jb in range(blocks_per_step):
        m = sp * blocks_per_step + jb
        kblk = jnp.concatenate([kp[jb * pages_per_block + u][...] for u in range(pages_per_block)], axis=1)
        vblk = jnp.concatenate([vp[jb * pages_per_block + u][...] for u in range(pages_per_block)], axis=1)
        ksum = jnp.sum(kblk, axis=1, keepdims=True)
        kmt = jnp.where(kmcol == m, ksum * (1.0 / MOBA_BLOCK), kmt)
        kpos = kcol + (m * MOBA_BLOCK).astype(_F32)
        s = jnp.dot(qb, kblk.astype(_BF16), preferred_element_type=_F32) - slope * (qpos - kpos)
        mb = jnp.max(s, axis=1, keepdims=True)
        p = jnp.exp(s - mb)
        oall_ref[m] = _nt_dot(p.astype(_BF16), vblk.astype(_BF16))
        here = gcol == m
        mtab = jnp.where(here, mb, mtab)
        ltab = jnp.where(here, jnp.sum(p, axis=1, keepdims=True), ltab)
    kmt_ref[...] = kmt
    mtab_ref[...] = mtab
    ltab_ref[...] = ltab

    @pl.when(sp == n_steps - 1)
    def _():
        n_blocks = n_steps * blocks_per_step
        ocol = lax.broadcasted_iota(jnp.int32, (rows, PAGE_SIZE), 1).astype(_F32)
        dist = (qpos - (n_blocks * MOBA_BLOCK)) - ocol
        s = _nt_dot(qb, kown_ref[0]) - slope * dist
        s = jnp.where(dist >= 0.0, s, NEG_INF)
        m_own = jnp.max(s, axis=1, keepdims=True)
        p = jnp.exp(s - m_own)
        l_own = jnp.sum(p, axis=1, keepdims=True)
        o_own = jnp.dot(p.astype(_BF16), vown_ref[0], preferred_element_type=_F32)

        gate = jnp.dot(qm, kmt_ref[...], precision=lax.Precision.HIGHEST, preferred_element_type=_F32)
        picked = _top3_select(gate, gcol < n_blocks, 1) > 0.0
        mtab = mtab_ref[...]
        m_all = jnp.maximum(m_own, jnp.max(jnp.where(picked, mtab, NEG_INF), axis=1, keepdims=True))
        w = jnp.where(picked, jnp.exp(mtab - m_all), 0.0)
        w_own = jnp.exp(m_own - m_all)
        l_all = w_own * l_own + jnp.sum(w * ltab_ref[...], axis=1, keepdims=True)

        def merge(mi, acc):
            wm = jnp.sum(jnp.where(gcol == mi, w, 0.0), axis=1, keepdims=True)
            return acc + wm * oall_ref[mi]

        acc = lax.fori_loop(0, n_blocks, merge, w_own * o_own)
        o_ref[0] = _sample_out(acc / l_all).astype(o_ref.dtype)


def _page_specs(layer, page_of, pps=PAGES_PER_STEP):
    def spec(j):
        return pl.BlockSpec((None, None, KV_DIM, PAGE_SIZE),
                            lambda b, s, pt, *_: (layer, pt[b, page_of(s, j)], 0, 0))
    return [spec(j) for j in range(pps)] * 2


def _moba_sample(page_table, slope_rows, qpos_rows, q_rows, k_own, v_own, cache_kt, cache_vt, *, layer):
    dec_b, n_pages = page_table.shape
    pps = MOBA_PAGES_PER_STEP if n_pages % MOBA_PAGES_PER_STEP == 0 else PAGES_PER_STEP
    n_steps = n_pages // pps
    rows = q_rows.shape[1] * N_KV_HEADS
    n_blocks = n_pages * PAGE_SIZE // MOBA_BLOCK
    grid_spec = pltpu.PrefetchScalarGridSpec(
        num_scalar_prefetch=1,
        grid=(dec_b, n_steps),
        in_specs=[
            pl.BlockSpec((rows, 1), lambda b, s, pt: (0, 0)),
            pl.BlockSpec((rows, 1), lambda b, s, pt: (0, 0)),
            pl.BlockSpec((1, q_rows.shape[1], KV_DIM), lambda b, s, pt: (b, 0, 0)),
            pl.BlockSpec((1, PAGE_SIZE, KV_DIM), lambda b, s, pt: (b, 0, 0)),
            pl.BlockSpec((1, PAGE_SIZE, KV_DIM), lambda b, s, pt: (b, 0, 0)),
        ] + _page_specs(layer, lambda s, j: s * pps + j, pps),
        out_specs=pl.BlockSpec((1, q_rows.shape[1], KV_DIM), lambda b, s, pt: (b, 0, 0)),
        scratch_shapes=[pltpu.VMEM((n_blocks, rows, KV_DIM), _F32),
                        pltpu.VMEM((rows, GATE_LANES), _F32),
                        pltpu.VMEM((rows, GATE_LANES), _F32),
                        pltpu.VMEM((KV_DIM, GATE_LANES), _F32)],
    )
    return pl.pallas_call(
        functools.partial(_moba_sample_body, n_steps=n_steps, pps=pps),
        grid_spec=grid_spec,
        out_shape=jax.ShapeDtypeStruct(q_rows.shape, _BF16),
        compiler_params=pltpu.CompilerParams(dimension_semantics=("arbitrary", "arbitrary"),
                                             vmem_limit_bytes=VMEM_LIMIT),
        name="moba_sample",
    )(page_table, slope_rows, qpos_rows, q_rows, k_own, v_own,
      *([cache_kt] * pps), *([cache_vt] * pps))


def _stick_pages(qb, kp, vp, tri, acc_ref, carry_ref, alive_ref):
    for j in range(PAGES_PER_STEP):
        @pl.when(alive_ref[0] > 0)
        def _():
            z = jnp.dot(qb, kp[j][...].astype(_BF16), preferred_element_type=_F32)
            pv, carry = _stick_chunk(z, None, carry_ref[...], tri, vc_nt=vp[j][...].astype(_BF16))
            acc_ref[...] += pv
            carry_ref[...] = carry
            alive_ref[0] = (jnp.max(carry) > STICK_DEAD).astype(jnp.int32)


def _stick_sample_first_body(pt_ref, tok_ref, q_ref, kown_ref, vown_ref, *rest):
    kp = rest[:PAGES_PER_STEP]
    vp = rest[PAGES_PER_STEP:2 * PAGES_PER_STEP]
    o_ref, acc_ref, carry_ref, alive_ref = rest[2 * PAGES_PER_STEP:]
    qb = _sample_rows(q_ref, SCALE * LOG2E).astype(_BF16)
    rows = qb.shape[0]
    tri = _stick_tri()
    kcol = lax.broadcasted_iota(jnp.int32, (rows, STICK_KB), 1).astype(_F32)
    causal = kcol < tok_ref[...]
    z = _nt_dot(qb, kown_ref[0])
    pv, carry = _stick_chunk(z, causal, jnp.zeros((rows, STICK_KB), _F32), tri, vc=vown_ref[0])
    acc_ref[0] = pv
    carry_ref[0] = carry
    alive_ref[0] = (jnp.max(carry) > STICK_DEAD).astype(jnp.int32)
    _stick_pages(qb, kp, vp, tri, acc_ref.at[0], carry_ref.at[0], alive_ref)
    o_ref[0] = _sample_out(acc_ref[0]).astype(o_ref.dtype)


def _stick_sample_rest_body(pt_ref, live_ref, q_ref, acc_in_ref, carry_in_ref, *rest, n_steps):
    kp = rest[:PAGES_PER_STEP]
    vp = rest[PAGES_PER_STEP:2 * PAGES_PER_STEP]
    o_ref, acc_ref, carry_ref, alive_ref = rest[2 * PAGES_PER_STEP:]
    sp = pl.program_id(1)
    qb = _sample_rows(q_ref, SCALE * LOG2E).astype(_BF16)

    @pl.when(sp == 0)
    def _():
        acc_ref[...] = acc_in_ref[0]
        carry_ref[...] = carry_in_ref[0]
        alive_ref[0] = live_ref[pl.program_id(0)]

    _stick_pages(qb, kp, vp, _stick_tri(), acc_ref, carry_ref, alive_ref)

    @pl.when(sp == n_steps - 1)
    def _():
        o_ref[0] = _sample_out(acc_ref[...]).astype(o_ref.dtype)


def _stick_sample(page_table, tok_rows, q_rows, k_own, v_own, cache_kt, cache_vt, *, layer):
    dec_b, n_pages = page_table.shape
    n_steps = n_pages // PAGES_PER_STEP
    qr = q_rows.shape[1]
    rows = qr * N_KV_HEADS
    per_b = lambda b, s, *_: (b, 0, 0)
    params = pltpu.CompilerParams(dimension_semantics=("arbitrary", "arbitrary"),
                                  vmem_limit_bytes=VMEM_LIMIT)
    pages = [cache_kt] * PAGES_PER_STEP + [cache_vt] * PAGES_PER_STEP
    o_rows, acc, carry = pl.pallas_call(
        _stick_sample_first_body,
        grid_spec=pltpu.PrefetchScalarGridSpec(
            num_scalar_prefetch=1,
            grid=(dec_b, 1),
            in_specs=[
                pl.BlockSpec((rows, 1), lambda b, s, *_: (0, 0)),
                pl.BlockSpec((1, qr, KV_DIM), per_b),
                pl.BlockSpec((1, PAGE_SIZE, KV_DIM), per_b),
                pl.BlockSpec((1, PAGE_SIZE, KV_DIM), per_b),
            ] + _page_specs(layer, lambda s, j: n_pages - 1 - j),
            out_specs=[pl.BlockSpec((1, qr, KV_DIM), per_b),
                       pl.BlockSpec((1, rows, KV_DIM), per_b),
                       pl.BlockSpec((1, rows, STICK_KB), per_b)],
            scratch_shapes=[pltpu.SMEM((1,), jnp.int32)],
        ),
        out_shape=[jax.ShapeDtypeStruct(q_rows.shape, _BF16),
                   jax.ShapeDtypeStruct((dec_b, rows, KV_DIM), _F32),
                   jax.ShapeDtypeStruct((dec_b, rows, STICK_KB), _F32)],
        compiler_params=params,
        name="stick_sample_first",
    )(page_table, tok_rows, q_rows, k_own, v_own, *pages)
    if n_steps == 1:
        return o_rows
    live = (jnp.max(carry, axis=(1, 2)) > STICK_DEAD).astype(jnp.int32)
    pt_rest = jnp.where(live[:, None] > 0, page_table, page_table[0, 0])
    return pl.pallas_call(
        functools.partial(_stick_sample_rest_body, n_steps=n_steps - 1),
        grid_spec=pltpu.PrefetchScalarGridSpec(
            num_scalar_prefetch=2,
            grid=(dec_b, n_steps - 1),
            in_specs=[
                pl.BlockSpec((1, qr, KV_DIM), per_b),
                pl.BlockSpec((1, rows, KV_DIM), per_b),
                pl.BlockSpec((1, rows, STICK_KB), per_b),
            ] + _page_specs(layer, lambda s, j: n_pages - 1 - ((s + 1) * PAGES_PER_STEP + j)),
            out_specs=pl.BlockSpec((1, qr, KV_DIM), per_b),
            scratch_shapes=[pltpu.VMEM((rows, KV_DIM), _F32),
                            pltpu.VMEM((rows, STICK_KB), _F32),
                            pltpu.SMEM((1,), jnp.int32)],
        ),
        out_shape=jax.ShapeDtypeStruct(q_rows.shape, _BF16),
        compiler_params=params,
        name="stick_sample_rest",
    )(pt_rest, live, q_rows, acc, carry, *pages)


def _ffn_body(*refs, tm, tiles_per_seq, sample, final):
    it = iter(refs)
    x_ref, o_ref, wo_ref, g_ref, wup_ref, cw_ref, cb_ref, wdn_ref = (next(it) for _ in range(8))
    s1_ref = s2_ref = tok_ref = gfin_ref = None
    if sample:
        s1_ref, s2_ref, tok_ref = next(it), next(it), next(it)
    if final:
        gfin_ref = next(it)
    xo_ref, u_ref = next(it), next(it)
    y_ref = next(it) if final else None
    ubuf, carry, abuf = next(it), next(it), next(it)

    i = pl.program_id(0)
    x1 = x_ref[...] + jnp.dot(o_ref[...], wo_ref[...], preferred_element_type=_F32)
    ms = jnp.mean(x1 * x1, axis=-1, keepdims=True)
    h = ((x1 * lax.rsqrt(ms + RMS_EPS)) * g_ref[...]).astype(_BF16)
    xo_ref[...] = x1

    if not sample:
        @pl.when(i % tiles_per_seq == 0)
        def _():
            carry[...] = jnp.zeros(carry.shape, _F32)
    ubuf[:, 0:8, :] = jnp.zeros((ubuf.shape[0], 8, FF_CHUNK), _F32)

    def conv(jj, slot):
        u = jnp.dot(h, wup_ref[jj], preferred_element_type=_F32)
        buf = ubuf.at[slot]
        buf[8:8 + tm, :] = u
        if sample:
            u_ref[jj] = u
            tok = tok_ref[...]
            u1 = jnp.where(tok < 1.0, s1_ref[jj], buf[7:7 + tm, :])
            u2 = jnp.where(tok < 2.0, s2_ref[jj], buf[6:6 + tm, :])
        else:
            buf[6:8, :] = carry[jj, 6:8, :]
            u1 = buf[7:7 + tm, :]
            u2 = buf[6:6 + tm, :]
            carry[jj, 6:8, :] = u[tm - 2:tm, :]
        w = cw_ref[jj]
        return cb_ref[jj] + w[0:1, :] * u2 + w[1:2, :] * u1 + w[2:3, :] * u

    for j in range(N_FF_CHUNKS):
        cg = conv(j, 2 * (j % 2))
        cv = conv(j + N_FF_CHUNKS, 2 * (j % 2) + 1)
        act = (cg / (1.0 + jnp.exp(-cg))) * cv
        abuf[:, j * FF_CHUNK:(j + 1) * FF_CHUNK] = act.astype(_BF16)
    xo_ref[...] += jnp.dot(abuf[...], wdn_ref[...], preferred_element_type=_F32)
    if not sample:
        u_ref[...] = carry[...]
    if final:
        xo = xo_ref[...]
        ms2 = jnp.mean(xo * xo, axis=-1, keepdims=True)
        y_ref[...] = (xo * lax.rsqrt(ms2 + RMS_EPS)) * gfin_ref[...]


def _oproj_ffn(x, o, wo, g, wup, cw, cb, wdn, *, tm, seq, sample_state=None, g_final=None):
    t = x.shape[0]
    sample = sample_state is not None
    final = g_final is not None
    n_tiles = t // tm
    tiles_per_seq = max(seq // tm, 1)
    const2 = lambda i: (0, 0)
    const3 = lambda i: (0, 0, 0)
    once = pl.Buffered(1)
    in_specs = [
        pl.BlockSpec((tm, D_MODEL), lambda i: (i, 0)),
        pl.BlockSpec((tm, D_MODEL), lambda i: (i, 0)),
        pl.BlockSpec(wo.shape, const2, pipeline_mode=once),
        pl.BlockSpec((1, D_MODEL), const2),
        pl.BlockSpec(wup.shape, const3, pipeline_mode=once),
        pl.BlockSpec(cw.shape, const3),
        pl.BlockSpec(cb.shape, const3),
        pl.BlockSpec(wdn.shape, const2, pipeline_mode=once),
    ]
    args = [x, o, wo, g, wup, cw, cb, wdn]
    if sample:
        s1, s2, tok = sample_state
        in_specs += [pl.BlockSpec(s1.shape, const3), pl.BlockSpec(s2.shape, const3),
                     pl.BlockSpec(tok.shape, const2)]
        args += [s1, s2, tok]
    if final:
        in_specs.append(pl.BlockSpec((1, D_MODEL), const2))
        args.append(g_final)
    out_shape = [jax.ShapeDtypeStruct((t, D_MODEL), _F32)]
    out_specs = [pl.BlockSpec((tm, D_MODEL), lambda i: (i, 0))]
    if sample:
        out_shape.append(jax.ShapeDtypeStruct((2 * N_FF_CHUNKS, t, FF_CHUNK), _F32))
        out_specs.append(pl.BlockSpec((2 * N_FF_CHUNKS, tm, FF_CHUNK), lambda i: (0, i, 0)))
    else:
        n_seq = t // seq
        out_shape.append(jax.ShapeDtypeStruct((n_seq, 2 * N_FF_CHUNKS, 8, FF_CHUNK), _F32))
        out_specs.append(pl.BlockSpec((None, 2 * N_FF_CHUNKS, 8, FF_CHUNK),
                                      lambda i: (i // tiles_per_seq, 0, 0, 0)))
    if final:
        out_shape.append(jax.ShapeDtypeStruct((t, D_MODEL), _F32))
        out_specs.append(pl.BlockSpec((tm, D_MODEL), lambda i: (i, 0)))
    return pl.pallas_call(
        functools.partial(_ffn_body, tm=tm, tiles_per_seq=tiles_per_seq, sample=sample, final=final),
        grid=(n_tiles,),
        in_specs=in_specs,
        out_specs=out_specs,
        out_shape=out_shape,
        scratch_shapes=[pltpu.VMEM((4, tm + 8, FF_CHUNK), _F32),
                        pltpu.VMEM((2 * N_FF_CHUNKS, 8, FF_CHUNK), _F32),
                        pltpu.VMEM((tm, D_FF), _BF16)],
        compiler_params=pltpu.CompilerParams(dimension_semantics=("arbitrary",),
                                             vmem_limit_bytes=VMEM_LIMIT),
        name="oproj_ffn_sample" if sample else "oproj_ffn",
    )(*args)


def _chunk_cols(a):
    lead = a.shape[:-2]
    r = a.shape[-2]
    a = a.reshape(lead + (r, 2 * N_FF_CHUNKS, FF_CHUNK))
    return jnp.moveaxis(a, -2, -3)


def _unchunk_cols(a):
    a = jnp.moveaxis(a, -3, -2)
    return a.reshape(a.shape[:-2] + (2 * D_FF,))


def _swap_halves(w):
    s = w.shape
    return w.reshape(s[:-1] + (s[-1] // SLAB, 2, HEAD_DIM))[..., ::-1, :].reshape(s)


def _bf16_pieces(x):
    hi = x.astype(_BF16).astype(_F32)
    mid = (x - hi).astype(_BF16).astype(_F32)
    lo = (x - hi - mid).astype(_BF16).astype(_F32)
    return hi, mid, lo


def kernel(x_prompt, x_sample, cache_k, cache_v, state_conv, page_table, g_attn, w_qkv, w_o,
           g_ffn, w_up, conv_w, conv_b, w_down, g_final):
    depth = w_qkv.shape[0]
    batch, seq, _ = x_prompt.shape
    dec_b, dec_s, _ = x_sample.shape
    n_pool = cache_k.shape[1]
    n_pages = page_table.shape[1]
    past = n_pages * PAGE_SIZE
    tp = batch * seq
    ts = dec_b * dec_s
    nb = seq // MOBA_BLOCK

    wq, wk, wv = (w_qkv[:, :, :D_MODEL], w_qkv[:, :, D_MODEL:D_MODEL + KV_DIM],
                  w_qkv[:, :, D_MODEL + KV_DIM:])
    wqkv = jnp.concatenate([wq, _swap_halves(wq), wk, wv, _swap_halves(wk), _swap_halves(wv)],
                           axis=2).astype(_BF16)
    wo = w_o.astype(_BF16)
    wup = _chunk_cols(w_up).astype(_BF16)
    cw = _chunk_cols(conv_w)
    cb = _chunk_cols(conv_b[:, None, :])
    wdn = w_down.astype(_BF16)
    g_attn2 = g_attn[:, None, :]
    g_ffn2 = g_ffn[:, None, :]
    g_fin2 = g_final[None, :]

    hidx = jnp.arange(1, N_HEADS + 1, dtype=_F32)
    slopes = jnp.exp2(-8.0 * hidx / N_HEADS)
    assert nb <= SLOPE_LANE0 - FLAG_LANE0 and past % MOBA_BLOCK == 0 and dec_s <= PAGE_SIZE
    pieces = jnp.stack(_bf16_pieces(slopes * LOG2E) * 2, axis=1)
    stab = jnp.zeros((N_HEADS, 8, SLAB), _F32).at[:, :, SLOPE_LANE0:SLOPE_LANE0 + 6].set(pieces[:, None, :])
    slope_rows = jnp.repeat(slopes, dec_s)[:, None]
    tok_rows = jnp.tile(jnp.arange(dec_s, dtype=_F32), N_HEADS)[:, None]
    qpos_rows = tok_rows + float(past)
    tok_seq = jnp.tile(jnp.arange(dec_s, dtype=_F32), dec_b)[:, None]

    cache_kt = cache_k.transpose(0, 1, 3, 4, 2).reshape(depth, n_pool, KV_DIM, PAGE_SIZE)
    cache_vt = cache_v.transpose(0, 1, 3, 4, 2).reshape(depth, n_pool, KV_DIM, PAGE_SIZE)

    st = state_conv
    zero = jnp.zeros_like(st[:, :, :1])
    s1 = jnp.concatenate([st[:, :, 1:2], zero, zero, zero][:dec_s], axis=2)
    s2 = jnp.concatenate([st[:, :, 0:1], st[:, :, 1:2], zero, zero][:dec_s], axis=2)
    s1 = _chunk_cols(s1.reshape(depth, ts, 2 * D_FF))
    s2 = _chunk_cols(s2.reshape(depth, ts, 2 * D_FF))

    xp = x_prompt.reshape(tp, D_MODEL)
    xs = x_sample.reshape(ts, D_MODEL)
    kp_l, vp_l, ks_l, vs_l, cp_l, cs_l = [], [], [], [], [], []
    yp = ys = None
    pad_own = jnp.zeros((dec_b, PAGE_SIZE - dec_s, KV_DIM), _BF16)
    for l in range(depth):
        last = l == depth - 1
        gfin = g_fin2 if last else None
        moba = l % 2 == 0
        q, k, v, kslab, vslab, *km = _rms_qkv(xp, g_attn2[l], wqkv[l], tm=512, with_kmean=moba)
        if moba:
            kmt = km[0].reshape(batch, nb, N_KV_HEADS, HEAD_DIM).transpose(0, 2, 1, 3)
            kmt = jnp.pad(kmt, ((0, 0), (0, 0), (0, SLOPE_LANE0 - FLAG_LANE0 - nb),
                                (0, SLAB - HEAD_DIM)))
            knorm = jnp.sqrt(jnp.max(km[1].reshape(batch, nb, N_KV_HEADS, HEAD_DIM), axis=(1, 3)))
            o = _moba_prompt(knorm.reshape(batch * N_KV_HEADS), stab, q, kslab, vslab, kmt,
                             batch=batch, seq=seq)
        else:
            o = _stick_prompt(q, kslab, vslab, batch=batch, seq=seq)
        outs = _oproj_ffn(xp, o, wo[l], g_ffn2[l], wup[l], cw[l], cb[l], wdn[l],
                          tm=512, seq=seq, g_final=gfin)
        xp, cst = outs[0], outs[1]
        if last:
            yp = outs[2]
        kp_l.append(k.reshape(batch, seq, N_KV_HEADS, HEAD_DIM))
        vp_l.append(v.reshape(batch, seq, N_KV_HEADS, HEAD_DIM))
        cp_l.append(_unchunk_cols(cst[:, :, 6:8, :]))
        qs, ksn, vsn, _, _ = _rms_qkv(xs, g_attn2[l], wqkv[l], tm=ts, with_kmean=False)
        qnat = qs[:, :, :HEAD_DIM].reshape(N_KV_HEADS, GROUP, dec_b, dec_s, HEAD_DIM)
        q_rows = qnat.transpose(2, 1, 3, 0, 4).reshape(dec_b, GROUP * dec_s, KV_DIM)
        k_own = jnp.concatenate([ksn.astype(_BF16).reshape(dec_b, dec_s, KV_DIM), pad_own], axis=1)
        v_own = jnp.concatenate([vsn.astype(_BF16).reshape(dec_b, dec_s, KV_DIM), pad_own], axis=1)
        if moba:
            o_rows = _moba_sample(page_table, slope_rows, qpos_rows, q_rows, k_own, v_own,
                                  cache_kt, cache_vt, layer=l)
        else:
            o_rows = _stick_sample(page_table, tok_rows, q_rows, k_own, v_own, cache_kt, cache_vt, layer=l)
        os_ = o_rows.reshape(dec_b, GROUP, dec_s, N_KV_HEADS, HEAD_DIM).transpose(0, 2, 3, 1, 4)
        outs = _oproj_ffn(xs, os_.reshape(ts, D_MODEL), wo[l], g_ffn2[l], wup[l], cw[l], cb[l], wdn[l],
                          tm=ts, seq=dec_s, sample_state=(s1[l], s2[l], tok_seq), g_final=gfin)
        xs, u_all = outs[0], outs[1]
        if last:
            ys = outs[2]
        ks_l.append(ksn.reshape(dec_b, dec_s, N_KV_HEADS, HEAD_DIM))
        vs_l.append(vsn.reshape(dec_b, dec_s, N_KV_HEADS, HEAD_DIM))
        u_full = _unchunk_cols(u_all).reshape(dec_b, dec_s, 2 * D_FF)
        cs_l.append(u_full[:, dec_s - (CONV_W - 1):, :])
    return (yp.reshape(batch, seq, D_MODEL), ys.reshape(dec_b, dec_s, D_MODEL),
            jnp.stack(kp_l), jnp.stack(vp_l), jnp.stack(ks_l), jnp.stack(vs_l),
            jnp.stack(cp_l), jnp.stack(cs_l))
```

```python
import functools

import jax
import jax.numpy as jnp
from jax import lax
from jax.experimental import pallas as pl
from jax.experimental.pallas import tpu as pltpu

D_MODEL = 1024
N_HEADS = 16
HEAD_DIM = 64
N_KV_HEADS = 4
GROUP = N_HEADS // N_KV_HEADS
KV_DIM = N_KV_HEADS * HEAD_DIM
SLAB = 2 * HEAD_DIM
MOBA_BLOCK = 256
MOBA_TOPK = 3
D_FF = 2816
CONV_W = 3
PAGE_SIZE = 128
RMS_EPS = 1e-6
NEG_INF = -1e30
SCALE = HEAD_DIM ** -0.5

FF_CHUNK = 256
N_FF_CHUNKS = D_FF // FF_CHUNK
STICK_KB = 128
LOG2E = 1.4426950408889634
STICK_DEAD = -110.0 * LOG2E
MOBA_DEAD = 152.0
NORM_SLACK = 1.04
GATE_LANES = 128
FLAG_LANE0 = HEAD_DIM
SLOPE_LANE0 = 96
PAGES_PER_STEP = 8
MOBA_PAGES_PER_STEP = 16
VMEM_LIMIT = 56 * 1024 * 1024
TQ = 256
STICK_TQ = 256

_BF16 = jnp.bfloat16
_F32 = jnp.float32


def _nt_dot(a, b):
    return lax.dot_general(a, b, (((1,), (1,)), ((), ())), preferred_element_type=_F32)


def _lane_group(shape):
    return lax.broadcasted_iota(jnp.int32, shape, len(shape) - 1) >> 6


def _qkv_body(x_ref, g_ref, w_ref, q_ref, k_ref, v_ref, ks_ref, vs_ref, *km_ref, tm):
    x = x_ref[...]
    ms = jnp.mean(x * x, axis=-1, keepdims=True)
    h = (x * lax.rsqrt(ms + RMS_EPS)) * g_ref[...]
    qkv = jnp.dot(h.astype(_BF16), w_ref[...], preferred_element_type=_F32)
    for j in range(N_HEADS // 2):
        pair = qkv[:, j * SLAB:(j + 1) * SLAB]
        q_ref[2 * j] = pair
        q_ref[2 * j + 1] = pltpu.roll(pair, HEAD_DIM, 1)
    k = qkv[:, D_MODEL:D_MODEL + KV_DIM]
    v = qkv[:, D_MODEL + KV_DIM:D_MODEL + 2 * KV_DIM]
    k_ref[...] = k
    v_ref[...] = v
    for j in range(N_KV_HEADS // 2):
        for src, dst in ((k, ks_ref), (v, vs_ref)):
            pair = src[:, j * SLAB:(j + 1) * SLAB]
            dst[2 * j] = pair.astype(_BF16)
            dst[2 * j + 1] = pltpu.roll(pair, HEAD_DIM, 1).astype(_BF16)
    if km_ref:
        li = lax.broadcasted_iota(jnp.int32, (KV_DIM, KV_DIM), 0) >> 6
        lj = lax.broadcasted_iota(jnp.int32, (KV_DIM, KV_DIM), 1) >> 6
        k2 = jnp.dot((k * k).astype(_BF16), jnp.where(li == lj, 1.0, 0.0).astype(_BF16),
                     preferred_element_type=_F32)
        for j in range(tm // MOBA_BLOCK):
            blk = slice(j * MOBA_BLOCK, (j + 1) * MOBA_BLOCK)
            km_ref[0][j] = jnp.mean(k[blk], axis=0, keepdims=True)
            km_ref[1][j] = jnp.max(k2[blk], axis=0, keepdims=True)


def _rms_qkv(x, g, w, *, tm, with_kmean):
    t = x.shape[0]
    out_shape = [
        jax.ShapeDtypeStruct((N_HEADS, t, SLAB), _F32),
        jax.ShapeDtypeStruct((t, KV_DIM), _F32),
        jax.ShapeDtypeStruct((t, KV_DIM), _F32),
        jax.ShapeDtypeStruct((N_KV_HEADS, t, SLAB), _BF16),
        jax.ShapeDtypeStruct((N_KV_HEADS, t, SLAB), _BF16),
    ]
    row = pl.BlockSpec((tm, KV_DIM), lambda i: (i, 0))
    slab = pl.BlockSpec((N_KV_HEADS, tm, SLAB), lambda i: (0, i, 0))
    out_specs = [pl.BlockSpec((N_HEADS, tm, SLAB), lambda i: (0, i, 0)), row, row, slab, slab]
    if with_kmean:
        for _ in range(2):
            out_shape.append(jax.ShapeDtypeStruct((t // MOBA_BLOCK, 1, KV_DIM), _F32))
            out_specs.append(pl.BlockSpec((tm // MOBA_BLOCK, 1, KV_DIM), lambda i: (i, 0, 0)))
    return pl.pallas_call(
        functools.partial(_qkv_body, tm=tm),
        grid=(t // tm,),
        in_specs=[
            pl.BlockSpec((tm, D_MODEL), lambda i: (i, 0)),
            pl.BlockSpec((1, D_MODEL), lambda i: (0, 0)),
            pl.BlockSpec(w.shape, lambda i: (0, 0)),
        ],
        out_specs=out_specs,
        out_shape=out_shape,
        compiler_params=pltpu.CompilerParams(dimension_semantics=("arbitrary",),
                                             vmem_limit_bytes=VMEM_LIMIT),
        name="rms_qkv",
    )(x, g, w)


def _top3_select(gate, eligible, axis):
    idx = lax.broadcasted_iota(jnp.int32, gate.shape, axis).astype(_F32)
    neg = jnp.float32(-jnp.inf)
    gm = jnp.where(eligible, gate, neg)
    sel = jnp.zeros(gate.shape, _F32)
    for _ in range(MOBA_TOPK):
        mx = jnp.max(gm, axis=axis, keepdims=True)
        first = jnp.min(jnp.where(gm == mx, idx, jnp.float32(1e9)), axis=axis, keepdims=True)
        pick = jnp.logical_and(idx == first, mx > neg)
        sel = jnp.where(pick, 1.0, sel)
        gm = jnp.where(pick, neg, gm)
    return sel


def _stick_tri():
    j = lax.broadcasted_iota(jnp.int32, (2 * STICK_KB, 2 * STICK_KB), 0) & (STICK_KB - 1)
    s = lax.broadcasted_iota(jnp.int32, (2 * STICK_KB, 2 * STICK_KB), 1)
    return jnp.where(jnp.logical_or(s >= STICK_KB, j > s), 1.0, 0.0).astype(_BF16)


def _stick_chunk(z, causal, carry, tri, vc_nt=None, vc=None):
    nz = -z
    e = jnp.exp2(jnp.minimum(z, nz))
    log_keep = jnp.minimum(nz, 0.0) - jnp.log2(1.0 + e)
    log_sig = z + log_keep
    if causal is not None:
        log_keep = jnp.where(causal, log_keep, 0.0)
    hi = log_keep.astype(_BF16)
    lo = (log_keep - hi.astype(_F32)).astype(_BF16)
    ct = jnp.dot(jnp.concatenate([hi, lo], axis=1), tri, preferred_element_type=_F32)
    log_after = carry + ct[:, :STICK_KB]
    a = jnp.exp2(log_sig + log_after)
    if causal is not None:
        a = jnp.where(causal, a, 0.0)
    a = a.astype(_BF16)
    pv = _nt_dot(a, vc_nt) if vc is None else jnp.dot(a, vc, preferred_element_type=_F32)
    return pv, carry + ct[:, STICK_KB:]


def _pair_heads(acc_ref, o_ref, normalise):
    low = lax.broadcasted_iota(jnp.int32, acc_ref.shape[1:], 1) < HEAD_DIM
    for j in range(N_HEADS // 2):
        a0 = acc_ref[2 * j]
        a1 = acc_ref[2 * j + 1]
        out = jnp.where(low, a0, pltpu.roll(a1, HEAD_DIM, 1))
        if normalise:
            out = out / jnp.where(low, pltpu.roll(a0, HEAD_DIM, 1), a1)
        o_ref[:, j * SLAB:(j + 1) * SLAB] = out.astype(o_ref.dtype)


def _moba_prompt_body(kmax_ref, stab_ref, q_ref, k_ref, v_ref, km_ref, o_ref, qa_ref, m_ref, acc_ref, *, tq):
    i = pl.program_id(1)
    rows = GROUP * tq
    low = lax.broadcasted_iota(jnp.int32, (tq, SLAB), 1) < HEAD_DIM
    n_flags = SLOPE_LANE0 - FLAG_LANE0
    eligible = lax.broadcasted_iota(jnp.int32, (n_flags, tq), 0) < i
    krow = lax.broadcasted_iota(jnp.int32, (MOBA_BLOCK, SLAB), 0)
    klane = lax.broadcasted_iota(jnp.int32, (MOBA_BLOCK, SLAB), 1)
    klow = klane < HEAD_DIM
    one = jnp.ones((MOBA_BLOCK, SLAB), _BF16)

    def key_aux(block_offset, c):
        f = jnp.where(klane == FLAG_LANE0 + c, NEG_INF, 0.0)
        f = jnp.where(jnp.logical_and(klane >= SLOPE_LANE0, klane < SLOPE_LANE0 + 3), block_offset, f)
        f = jnp.where(jnp.logical_and(klane >= SLOPE_LANE0 + 3, klane < SLOPE_LANE0 + 6),
                      krow.astype(_F32), f)
        return f.astype(_BF16)

    def kv_head(n, _):
        km = km_ref[n]
        for g in range(GROUP):
            h = n * GROUP + g
            qh = q_ref[h]
            gate_t = lax.dot_general(km, qh, (((1,), (1,)), ((), ())),
                                     precision=lax.Precision.HIGHEST, preferred_element_type=_F32)
            sel_t = _top3_select(gate_t, eligible, 0)
            notsel_t = jnp.where(eligible, 1.0 - sel_t, 0.0)
            notsel = jnp.concatenate([jnp.zeros((FLAG_LANE0, tq), _F32), notsel_t,
                                      jnp.zeros((GATE_LANES - FLAG_LANE0 - n_flags, tq), _F32)],
                                     axis=0).T
            q2 = qh * (SCALE * LOG2E)
            qa_ref[h] = jnp.where(low, q2, notsel + stab_ref[h][0:1, :]).astype(_BF16)
            qn = jnp.sqrt(jnp.sum(jnp.where(low, q2 * q2, 0.0), axis=1, keepdims=True))
            m_ref[h] = jnp.broadcast_to(qn, (tq, SLAB))
        return 0

    lax.fori_loop(0, N_KV_HEADS, kv_head, 0)

    off = pl.multiple_of(i * MOBA_BLOCK, MOBA_BLOCK)
    f_own = key_aux(0.0, i)
    qrow = lax.broadcasted_iota(jnp.int32, (rows, MOBA_BLOCK), 0) & (tq - 1)
    kcol = lax.broadcasted_iota(jnp.int32, (rows, MOBA_BLOCK), 1)
    causal = qrow >= kcol
    keep = []
    for n in range(N_KV_HEADS):
        hs = pl.ds(n * GROUP, GROUP)
        kk = jnp.where(klow, k_ref[n, pl.ds(off, MOBA_BLOCK), :], f_own)
        vv = jnp.where(klow, v_ref[n, pl.ds(off, MOBA_BLOCK), :], one)
        s = _nt_dot(qa_ref[hs].reshape(rows, SLAB), kk)
        s = jnp.where(causal, s, NEG_INF)
        m0 = jnp.max(s, axis=1, keepdims=True)
        p = jnp.exp2(s - m0)
        qn = m_ref[hs].reshape(rows, SLAB)[:, 0:1]
        m_ref[hs] = jnp.broadcast_to(m0, (rows, SLAB)).reshape(GROUP, tq, SLAB)
        acc_ref[hs] = jnp.dot(p.astype(_BF16), vv, preferred_element_type=_F32).reshape(GROUP, tq, SLAB)
        slope_min = 2.0 ** (-8.0 * (n * GROUP + GROUP) / N_HEADS) * LOG2E * 0.999
        reach = (qn * (kmax_ref[pl.program_id(0) * N_KV_HEADS + n] * NORM_SLACK) + MOBA_DEAD - m0)
        blocks = jnp.max(reach) * (1.0 / (MOBA_BLOCK * slope_min)) - 1.0 / MOBA_BLOCK
        k_n = jnp.int32(0)
        for d in range(n_flags):
            k_n = k_n + (blocks >= float(d)).astype(jnp.int32)
        keep.append(k_n if n == 0 else jnp.maximum(k_n, keep[-1]))

    def past(c, _, groups):
        offc = pl.multiple_of(c * MOBA_BLOCK, MOBA_BLOCK)
        f_c = key_aux(((c - i) * MOBA_BLOCK).astype(_F32), c)
        for n in groups:
            hs = pl.ds(n * GROUP, GROUP)
            kk = jnp.where(klow, k_ref[n, pl.ds(offc, MOBA_BLOCK), :], f_c)
            vv = jnp.where(klow, v_ref[n, pl.ds(offc, MOBA_BLOCK), :], one)
            s = _nt_dot(qa_ref[hs].reshape(rows, SLAB), kk)
            m_old = m_ref[hs].reshape(rows, SLAB)
            m_new = jnp.maximum(m_old, jnp.max(s, axis=1, keepdims=True))
            alpha = jnp.exp2(m_old - m_new)
            p = jnp.exp2(s - jnp.concatenate([m_new, m_new], axis=1))
            pv = jnp.dot(p.astype(_BF16), vv, preferred_element_type=_F32)
            m_ref[hs] = m_new.reshape(GROUP, tq, SLAB)
            acc_ref[hs] = (alpha * acc_ref[hs].reshape(rows, SLAB) + pv).reshape(GROUP, tq, SLAB)
        return 0

    start = [jnp.maximum(i - k_n, 0) for k_n in keep]
    for n in range(N_KV_HEADS - 1, -1, -1):
        stop = start[n - 1] if n > 0 else i
        body = functools.partial(past, groups=tuple(range(n, N_KV_HEADS)))
        if n == 0:
            lax.fori_loop(start[n], stop, body, 0)
            continue
        pairs = (stop - start[n]) // 2

        def two(t, _, body=body, first=start[n]):
            body(first + 2 * t, 0)
            body(first + 2 * t + 1, 0)
            return 0

        lax.fori_loop(0, pairs, two, 0)
        lax.fori_loop(start[n] + 2 * pairs, stop, body, 0)
    _pair_heads(acc_ref, o_ref, normalise=True)


def _moba_prompt(kmax, stab, q, ks, vs, kmtab, *, batch, seq):
    tq = TQ
    nq = seq // tq
    return pl.pallas_call(
        functools.partial(_moba_prompt_body, tq=tq),
        grid=(batch, nq),
        in_specs=[
            pl.BlockSpec(memory_space=pltpu.SMEM),
            pl.BlockSpec(stab.shape, lambda b, i: (0, 0, 0)),
            pl.BlockSpec((N_HEADS, tq, SLAB), lambda b, i: (0, b * nq + i, 0)),
            pl.BlockSpec((N_KV_HEADS, seq, SLAB), lambda b, i: (0, b, 0)),
            pl.BlockSpec((N_KV_HEADS, seq, SLAB), lambda b, i: (0, b, 0)),
            pl.BlockSpec((None,) + kmtab.shape[1:], lambda b, i: (b, 0, 0, 0)),
        ],
        out_specs=pl.BlockSpec((tq, D_MODEL), lambda b, i: (b * nq + i, 0)),
        out_shape=jax.ShapeDtypeStruct((batch * seq, D_MODEL), _BF16),
        scratch_shapes=[pltpu.VMEM((N_HEADS, tq, SLAB), _BF16),
                        pltpu.VMEM((N_HEADS, tq, SLAB), _F32),
                        pltpu.VMEM((N_HEADS, tq, SLAB), _F32)],
        compiler_params=pltpu.CompilerParams(dimension_semantics=("arbitrary", "arbitrary"),
                                             vmem_limit_bytes=VMEM_LIMIT),
        name="moba_prompt",
    )(kmax, stab, q, ks, vs, kmtab)


def _stick_prompt_body(q_ref, k_ref, v_ref, o_ref, qa_ref, acc_ref, carry_ref, *, tq):
    i = pl.program_id(1)
    rows = GROUP * tq
    low = lax.broadcasted_iota(jnp.int32, q_ref.shape, 2) < HEAD_DIM
    qa_ref[...] = jnp.where(low, q_ref[...] * (SCALE * LOG2E), 0.0).astype(_BF16)
    acc_ref[...] = jnp.zeros(acc_ref.shape, _F32)
    carry_ref[...] = jnp.zeros(carry_ref.shape, _F32)
    tri = _stick_tri()
    qrow = lax.broadcasted_iota(jnp.int32, (rows, STICK_KB), 0) & (tq - 1)
    kcol = lax.broadcasted_iota(jnp.int32, (rows, STICK_KB), 1)
    rel = qrow - kcol
    chunks_per_tile = tq // STICK_KB

    def chunk(c, masked):
        off = pl.multiple_of(c * STICK_KB, STICK_KB)
        causal = (rel + (i * tq - c * STICK_KB) > 0) if masked else None
        top = jnp.float32(-jnp.inf)
        for n in range(N_KV_HEADS):
            hs = pl.ds(n * GROUP, GROUP)
            z = _nt_dot(qa_ref[hs].reshape(rows, SLAB), k_ref[n, pl.ds(off, STICK_KB), :])
            pv, carry = _stick_chunk(z, causal, carry_ref[hs].reshape(rows, STICK_KB), tri,
                                     vc=v_ref[n, pl.ds(off, STICK_KB), :])
            acc_ref[hs] += pv.reshape(GROUP, tq, SLAB)
            carry_ref[hs] = carry.reshape(GROUP, tq, STICK_KB)
            top = jnp.maximum(top, jnp.max(carry))
        return top

    top = jnp.float32(0.0)
    for d in range(chunks_per_tile):
        top = chunk((i + 1) * chunks_per_tile - 1 - d, True)

    def cond(state):
        c, alive = state
        return jnp.logical_and(c >= 0, alive > 0)

    def step(state):
        c, _ = state
        return c - 1, (chunk(c, False) > STICK_DEAD).astype(jnp.int32)

    lax.while_loop(cond, step, (i * chunks_per_tile - 1, (top > STICK_DEAD).astype(jnp.int32)))
    _pair_heads(acc_ref, o_ref, normalise=False)


def _stick_prompt(q, ks, vs, *, batch, seq):
    tq = STICK_TQ
    nq = seq // tq
    return pl.pallas_call(
        functools.partial(_stick_prompt_body, tq=tq),
        grid=(batch, nq),
        in_specs=[
            pl.BlockSpec((N_HEADS, tq, SLAB), lambda b, i: (0, b * nq + i, 0)),
            pl.BlockSpec((N_KV_HEADS, seq, SLAB), lambda b, i: (0, b, 0)),
            pl.BlockSpec((N_KV_HEADS, seq, SLAB), lambda b, i: (0, b, 0)),
        ],
        out_specs=pl.BlockSpec((tq, D_MODEL), lambda b, i: (b * nq + i, 0)),
        out_shape=jax.ShapeDtypeStruct((batch * seq, D_MODEL), _BF16),
        scratch_shapes=[pltpu.VMEM((N_HEADS, tq, SLAB), _BF16),
                        pltpu.VMEM((N_HEADS, tq, SLAB), _F32),
                        pltpu.VMEM((N_HEADS, tq, STICK_KB), _F32)],
        compiler_params=pltpu.CompilerParams(dimension_semantics=("arbitrary", "arbitrary"),
                                             vmem_limit_bytes=VMEM_LIMIT),
        name="stick_prompt",
    )(q, ks, vs)


def _sample_rows(q_ref, scale):
    q = q_ref[0] * scale
    grp = _lane_group(q.shape)
    return jnp.concatenate([jnp.where(grp == n, q, 0.0) for n in range(N_KV_HEADS)], axis=0)


def _sample_out(acc):
    r = acc.shape[0] // N_KV_HEADS
    grp = _lane_group((r, KV_DIM))
    out = jnp.zeros((r, KV_DIM), _F32)
    for n in range(N_KV_HEADS):
        out = jnp.where(grp == n, acc[n * r:(n + 1) * r], out)
    return out


def _moba_sample_body(pt_ref, slope_ref, qpos_ref, q_ref, kown_ref, vown_ref, *rest, n_steps, pps):
    kp = rest[:pps]
    vp = rest[pps:2 * pps]
    o_ref, oall_ref, mtab_ref, ltab_ref, kmt_ref = rest[2 * pps:]
    sp = pl.program_id(1)
    qm = _sample_rows(q_ref, SCALE)
    qb = qm.astype(_BF16)
    rows = qm.shape[0]
    slope = slope_ref[...]
    qpos = qpos_ref[...]
    gcol = lax.broadcasted_iota(jnp.int32, (rows, GATE_LANES), 1)
    kmcol = lax.broadcasted_iota(jnp.int32, (KV_DIM, GATE_LANES), 1)
    kcol = lax.broadcasted_iota(jnp.int32, (rows, MOBA_BLOCK), 1).astype(_F32)
    pages_per_block = MOBA_BLOCK // PAGE_SIZE
    blocks_per_step = pps // pages_per_block

    @pl.when(sp == 0)
    def _():
        for tab in (mtab_ref, ltab_ref, kmt_ref):
            tab[...] = jnp.zeros(tab.shape, _F32)

    kmt, mtab, ltab = kmt_ref[...], mtab_ref[...], ltab_ref[...]
    for jb in range(blocks_per_step):
        m = sp * blocks_per_step + jb
        kblk = jnp.concatenate([kp[jb * pages_per_block + u][...] for u in range(pages_per_block)], axis=1)
        vblk = jnp.concatenate([vp[jb * pages_per_block + u][...] for u in range(pages_per_block)], axis=1)
        ksum = jnp.sum(kblk, axis=1, keepdims=True)
        kmt = jnp.where(kmcol == m, ksum * (1.0 / MOBA_BLOCK), kmt)
        kpos = kcol + (m * MOBA_BLOCK).astype(_F32)
        s = jnp.dot(qb, kblk.astype(_BF16), preferred_element_type=_F32) - slope * (qpos - kpos)
        mb = jnp.max(s, axis=1, keepdims=True)
        p = jnp.exp(s - mb)
        oall_ref[m] = _nt_dot(p.astype(_BF16), vblk.astype(_BF16))
        here = gcol == m
        mtab = jnp.where(here, mb, mtab)
        ltab = jnp.where(here, jnp.sum(p, axis=1, keepdims=True), ltab)
    kmt_ref[...] = kmt
    mtab_ref[...] = mtab
    ltab_ref[...] = ltab

    @pl.when(sp == n_steps - 1)
    def _():
        n_blocks = n_steps * blocks_per_step
        ocol = lax.broadcasted_iota(jnp.int32, (rows, PAGE_SIZE), 1).astype(_F32)
        dist = (qpos - (n_blocks * MOBA_BLOCK)) - ocol
        s = _nt_dot(qb, kown_ref[0]) - slope * dist
        s = jnp.where(dist >= 0.0, s, NEG_INF)
        m_own = jnp.max(s, axis=1, keepdims=True)
        p = jnp.exp(s - m_own)
        l_own = jnp.sum(p, axis=1, keepdims=True)
        o_own = jnp.dot(p.astype(_BF16), vown_ref[0], preferred_element_type=_F32)

        gate = jnp.dot(qm, kmt_ref[...], precision=lax.Precision.HIGHEST, preferred_element_type=_F32)
        picked = _top3_select(gate, gcol < n_blocks, 1) > 0.0
        mtab = mtab_ref[...]
        m_all = jnp.maximum(m_own, jnp.max(jnp.where(picked, mtab, NEG_INF), axis=1, keepdims=True))
        w = jnp.where(picked, jnp.exp(mtab - m_all), 0.0)
        w_own = jnp.exp(m_own - m_all)
        l_all = w_own * l_own + jnp.sum(w * ltab_ref[...], axis=1, keepdims=True)

        def merge(mi, acc):
            wm = jnp.sum(jnp.where(gcol == mi, w, 0.0), axis=1, keepdims=True)
            return acc + wm * oall_ref[mi]

        acc = lax.fori_loop(0, n_blocks, merge, w_own * o_own)
        o_ref[0] = _sample_out(acc / l_all).astype(o_ref.dtype)


def _page_specs(layer, page_of, pps=PAGES_PER_STEP):
    def spec(j):
        return pl.BlockSpec((None, None, KV_DIM, PAGE_SIZE),
                            lambda b, s, pt, *_: (layer, pt[b, page_of(s, j)], 0, 0))
    return [spec(j) for j in range(pps)] * 2


def _moba_sample(page_table, slope_rows, qpos_rows, q_rows, k_own, v_own, cache_kt, cache_vt, *, layer):
    dec_b, n_pages = page_table.shape
    pps = MOBA_PAGES_PER_STEP if n_pages % MOBA_PAGES_PER_STEP == 0 else PAGES_PER_STEP
    n_steps = n_pages // pps
    rows = q_rows.shape[1] * N_KV_HEADS
    n_blocks = n_pages * PAGE_SIZE // MOBA_BLOCK
    grid_spec = pltpu.PrefetchScalarGridSpec(
        num_scalar_prefetch=1,
        grid=(dec_b, n_steps),
        in_specs=[
            pl.BlockSpec((rows, 1), lambda b, s, pt: (0, 0)),
            pl.BlockSpec((rows, 1), lambda b, s, pt: (0, 0)),
            pl.BlockSpec((1, q_rows.shape[1], KV_DIM), lambda b, s, pt: (b, 0, 0)),
            pl.BlockSpec((1, PAGE_SIZE, KV_DIM), lambda b, s, pt: (b, 0, 0)),
            pl.BlockSpec((1, PAGE_SIZE, KV_DIM), lambda b, s, pt: (b, 0, 0)),
        ] + _page_specs(layer, lambda s, j: s * pps + j, pps),
        out_specs=pl.BlockSpec((1, q_rows.shape[1], KV_DIM), lambda b, s, pt: (b, 0, 0)),
        scratch_shapes=[pltpu.VMEM((n_blocks, rows, KV_DIM), _F32),
                        pltpu.VMEM((rows, GATE_LANES), _F32),
                        pltpu.VMEM((rows, GATE_LANES), _F32),
                        pltpu.VMEM((KV_DIM, GATE_LANES), _F32)],
    )
    return pl.pallas_call(
        functools.partial(_moba_sample_body, n_steps=n_steps, pps=pps),
        grid_spec=grid_spec,
        out_shape=jax.ShapeDtypeStruct(q_rows.shape, _BF16),
        compiler_params=pltpu.CompilerParams(dimension_semantics=("arbitrary", "arbitrary"),
                                             vmem_limit_bytes=VMEM_LIMIT),
        name="moba_sample",
    )(page_table, slope_rows, qpos_rows, q_rows, k_own, v_own,
      *([cache_kt] * pps), *([cache_vt] * pps))


def _stick_pages(qb, kp, vp, tri, acc_ref, carry_ref, alive_ref):
    for j in range(PAGES_PER_STEP):
        @pl.when(alive_ref[0] > 0)
        def _():
            z = jnp.dot(qb, kp[j][...].astype(_BF16), preferred_element_type=_F32)
            pv, carry = _stick_chunk(z, None, carry_ref[...], tri, vc_nt=vp[j][...].astype(_BF16))
            acc_ref[...] += pv
            carry_ref[...] = carry
            alive_ref[0] = (jnp.max(carry) > STICK_DEAD).astype(jnp.int32)


def _stick_sample_first_body(pt_ref, tok_ref, q_ref, kown_ref, vown_ref, *rest):
    kp = rest[:PAGES_PER_STEP]
    vp = rest[PAGES_PER_STEP:2 * PAGES_PER_STEP]
    o_ref, acc_ref, carry_ref, alive_ref = rest[2 * PAGES_PER_STEP:]
    qb = _sample_rows(q_ref, SCALE * LOG2E).astype(_BF16)
    rows = qb.shape[0]
    tri = _stick_tri()
    kcol = lax.broadcasted_iota(jnp.int32, (rows, STICK_KB), 1).astype(_F32)
    causal = kcol < tok_ref[...]
    z = _nt_dot(qb, kown_ref[0])
    pv, carry = _stick_chunk(z, causal, jnp.zeros((rows, STICK_KB), _F32), tri, vc=vown_ref[0])
    acc_ref[0] = pv
    carry_ref[0] = carry
    alive_ref[0] = (jnp.max(carry) > STICK_DEAD).astype(jnp.int32)
    _stick_pages(qb, kp, vp, tri, acc_ref.at[0], carry_ref.at[0], alive_ref)
    o_ref[0] = _sample_out(acc_ref[0]).astype(o_ref.dtype)


def _stick_sample_rest_body(pt_ref, live_ref, q_ref, acc_in_ref, carry_in_ref, *rest, n_steps):
    kp = rest[:PAGES_PER_STEP]
    vp = rest[PAGES_PER_STEP:2 * PAGES_PER_STEP]
    o_ref, acc_ref, carry_ref, alive_ref = rest[2 * PAGES_PER_STEP:]
    sp = pl.program_id(1)
    qb = _sample_rows(q_ref, SCALE * LOG2E).astype(_BF16)

    @pl.when(sp == 0)
    def _():
        acc_ref[...] = acc_in_ref[0]
        carry_ref[...] = carry_in_ref[0]
        alive_ref[0] = live_ref[pl.program_id(0)]

    _stick_pages(qb, kp, vp, _stick_tri(), acc_ref, carry_ref, alive_ref)

    @pl.when(sp == n_steps - 1)
    def _():
        o_ref[0] = _sample_out(acc_ref[...]).astype(o_ref.dtype)


def _stick_sample(page_table, tok_rows, q_rows, k_own, v_own, cache_kt, cache_vt, *, layer):
    dec_b, n_pages = page_table.shape
    n_steps = n_pages // PAGES_PER_STEP
    qr = q_rows.shape[1]
    rows = qr * N_KV_HEADS
    per_b = lambda b, s, *_: (b, 0, 0)
    params = pltpu.CompilerParams(dimension_semantics=("arbitrary", "arbitrary"),
                                  vmem_limit_bytes=VMEM_LIMIT)
    pages = [cache_kt] * PAGES_PER_STEP + [cache_vt] * PAGES_PER_STEP
    o_rows, acc, carry = pl.pallas_call(
        _stick_sample_first_body,
        grid_spec=pltpu.PrefetchScalarGridSpec(
            num_scalar_prefetch=1,
            grid=(dec_b, 1),
            in_specs=[
                pl.BlockSpec((rows, 1), lambda b, s, *_: (0, 0)),
                pl.BlockSpec((1, qr, KV_DIM), per_b),
                pl.BlockSpec((1, PAGE_SIZE, KV_DIM), per_b),
                pl.BlockSpec((1, PAGE_SIZE, KV_DIM), per_b),
            ] + _page_specs(layer, lambda s, j: n_pages - 1 - j),
            out_specs=[pl.BlockSpec((1, qr, KV_DIM), per_b),
                       pl.BlockSpec((1, rows, KV_DIM), per_b),
                       pl.BlockSpec((1, rows, STICK_KB), per_b)],
            scratch_shapes=[pltpu.SMEM((1,), jnp.int32)],
        ),
        out_shape=[jax.ShapeDtypeStruct(q_rows.shape, _BF16),
                   jax.ShapeDtypeStruct((dec_b, rows, KV_DIM), _F32),
                   jax.ShapeDtypeStruct((dec_b, rows, STICK_KB), _F32)],
        compiler_params=params,
        name="stick_sample_first",
    )(page_table, tok_rows, q_rows, k_own, v_own, *pages)
    if n_steps == 1:
        return o_rows
    live = (jnp.max(carry, axis=(1, 2)) > STICK_DEAD).astype(jnp.int32)
    pt_rest = jnp.where(live[:, None] > 0, page_table, page_table[0, 0])
    rest = pl.pallas_call(
        functools.partial(_stick_sample_rest_body, n_steps=n_steps - 1),
        grid_spec=pltpu.PrefetchScalarGridSpec(
            num_scalar_prefetch=2,
            grid=(dec_b, n_steps - 1),
            in_specs=[
                pl.BlockSpec((1, qr, KV_DIM), per_b),
                pl.BlockSpec((1, rows, KV_DIM), per_b),
                pl.BlockSpec((1, rows, STICK_KB), per_b),
            ] + _page_specs(layer, lambda s, j: n_pages - 1 - ((s + 1) * PAGES_PER_STEP + j)),
            out_specs=pl.BlockSpec((1, qr, KV_DIM), per_b),
            scratch_shapes=[pltpu.VMEM((rows, KV_DIM), _F32),
                            pltpu.VMEM((rows, STICK_KB), _F32),
                            pltpu.SMEM((1,), jnp.int32)],
        ),
        out_shape=jax.ShapeDtypeStruct(q_rows.shape, _BF16),
        compiler_params=params,
        name="stick_sample_rest",
    )
    return lax.cond(jnp.max(live) > 0,
                    lambda: rest(pt_rest, live, q_rows, acc, carry, *pages),
                    lambda: o_rows)


def _ffn_body(*refs, tm, tiles_per_seq, sample, final):
    it = iter(refs)
    x_ref, o_ref, wo_ref, g_ref, wup_ref, cw_ref, cb_ref, wdn_ref = (next(it) for _ in range(8))
    s1_ref = s2_ref = tok_ref = gfin_ref = None
    if sample:
        s1_ref, s2_ref, tok_ref = next(it), next(it), next(it)
    if final:
        gfin_ref = next(it)
    xo_ref, u_ref = next(it), next(it)
    y_ref = next(it) if final else None
    ubuf, carry, abuf = next(it), next(it), next(it)

    i = pl.program_id(0)
    x1 = x_ref[...] + jnp.dot(o_ref[...], wo_ref[...], preferred_element_type=_F32)
    ms = jnp.mean(x1 * x1, axis=-1, keepdims=True)
    h = ((x1 * lax.rsqrt(ms + RMS_EPS)) * g_ref[...]).astype(_BF16)
    xo_ref[...] = x1

    if not sample:
        @pl.when(i % tiles_per_seq == 0)
        def _():
            carry[...] = jnp.zeros(carry.shape, _F32)
    ubuf[:, 0:8, :] = jnp.zeros((ubuf.shape[0], 8, FF_CHUNK), _F32)

    def conv(jj, slot):
        u = jnp.dot(h, wup_ref[jj], preferred_element_type=_F32)
        buf = ubuf.at[slot]
        buf[8:8 + tm, :] = u
        if sample:
            u_ref[jj] = u
            tok = tok_ref[...]
            u1 = jnp.where(tok < 1.0, s1_ref[jj], buf[7:7 + tm, :])
            u2 = jnp.where(tok < 2.0, s2_ref[jj], buf[6:6 + tm, :])
        else:
            buf[6:8, :] = carry[jj, 6:8, :]
            u1 = buf[7:7 + tm, :]
            u2 = buf[6:6 + tm, :]
            carry[jj, 6:8, :] = u[tm - 2:tm, :]
        w = cw_ref[jj]
        return cb_ref[jj] + w[0:1, :] * u2 + w[1:2, :] * u1 + w[2:3, :] * u

    for j in range(N_FF_CHUNKS):
        cg = conv(j, 2 * (j % 2))
        cv = conv(j + N_FF_CHUNKS, 2 * (j % 2) + 1)
        act = (cg / (1.0 + jnp.exp(-cg))) * cv
        abuf[:, j * FF_CHUNK:(j + 1) * FF_CHUNK] = act.astype(_BF16)
    xo_ref[...] += jnp.dot(abuf[...], wdn_ref[...], preferred_element_type=_F32)
    if not sample:
        u_ref[...] = carry[...]
    if final:
        xo = xo_ref[...]
        ms2 = jnp.mean(xo * xo, axis=-1, keepdims=True)
        y_ref[...] = (xo * lax.rsqrt(ms2 + RMS_EPS)) * gfin_ref[...]


def _oproj_ffn(x, o, wo, g, wup, cw, cb, wdn, *, tm, seq, sample_state=None, g_final=None):
    t = x.shape[0]
    sample = sample_state is not None
    final = g_final is not None
    n_tiles = t // tm
    tiles_per_seq = max(seq // tm, 1)
    const2 = lambda i: (0, 0)
    const3 = lambda i: (0, 0, 0)
    once = pl.Buffered(1)
    in_specs = [
        pl.BlockSpec((tm, D_MODEL), lambda i: (i, 0)),
        pl.BlockSpec((tm, D_MODEL), lambda i: (i, 0)),
        pl.BlockSpec(wo.shape, const2, pipeline_mode=once),
        pl.BlockSpec((1, D_MODEL), const2),
        pl.BlockSpec(wup.shape, const3, pipeline_mode=once),
        pl.BlockSpec(cw.shape, const3),
        pl.BlockSpec(cb.shape, const3),
        pl.BlockSpec(wdn.shape, const2, pipeline_mode=once),
    ]
    args = [x, o, wo, g, wup, cw, cb, wdn]
    if sample:
        s1, s2, tok = sample_state
        in_specs += [pl.BlockSpec(s1.shape, const3), pl.BlockSpec(s2.shape, const3),
                     pl.BlockSpec(tok.shape, const2)]
        args += [s1, s2, tok]
    if final:
        in_specs.append(pl.BlockSpec((1, D_MODEL), const2))
        args.append(g_final)
    out_shape = [jax.ShapeDtypeStruct((t, D_MODEL), _F32)]
    out_specs = [pl.BlockSpec((tm, D_MODEL), lambda i: (i, 0))]
    if sample:
        out_shape.append(jax.ShapeDtypeStruct((2 * N_FF_CHUNKS, t, FF_CHUNK), _F32))
        out_specs.append(pl.BlockSpec((2 * N_FF_CHUNKS, tm, FF_CHUNK), lambda i: (0, i, 0)))
    else:
        n_seq = t // seq
        out_shape.append(jax.ShapeDtypeStruct((n_seq, 2 * N_FF_CHUNKS, 8, FF_CHUNK), _F32))
        out_specs.append(pl.BlockSpec((None, 2 * N_FF_CHUNKS, 8, FF_CHUNK),
                                      lambda i: (i // tiles_per_seq, 0, 0, 0)))
    if final:
        out_shape.append(jax.ShapeDtypeStruct((t, D_MODEL), _F32))
        out_specs.append(pl.BlockSpec((tm, D_MODEL), lambda i: (i, 0)))
    return pl.pallas_call(
        functools.partial(_ffn_body, tm=tm, tiles_per_seq=tiles_per_seq, sample=sample, final=final),
        grid=(n_tiles,),
        in_specs=in_specs,
        out_specs=out_specs,
        out_shape=out_shape,
        scratch_shapes=[pltpu.VMEM((4, tm + 8, FF_CHUNK), _F32),
                        pltpu.VMEM((2 * N_FF_CHUNKS, 8, FF_CHUNK), _F32),
                        pltpu.VMEM((tm, D_FF), _BF16)],
        compiler_params=pltpu.CompilerParams(dimension_semantics=("arbitrary",),
                                             vmem_limit_bytes=VMEM_LIMIT),
        name="oproj_ffn_sample" if sample else "oproj_ffn",
    )(*args)


def _chunk_cols(a):
    lead = a.shape[:-2]
    r = a.shape[-2]
    a = a.reshape(lead + (r, 2 * N_FF_CHUNKS, FF_CHUNK))
    return jnp.moveaxis(a, -2, -3)


def _unchunk_cols(a):
    a = jnp.moveaxis(a, -3, -2)
    return a.reshape(a.shape[:-2] + (2 * D_FF,))


def _bf16_pieces(x):
    hi = x.astype(_BF16).astype(_F32)
    mid = (x - hi).astype(_BF16).astype(_F32)
    lo = (x - hi - mid).astype(_BF16).astype(_F32)
    return hi, mid, lo


def kernel(x_prompt, x_sample, cache_k, cache_v, state_conv, page_table, g_attn, w_qkv, w_o,
           g_ffn, w_up, conv_w, conv_b, w_down, g_final):
    depth = w_qkv.shape[0]
    batch, seq, _ = x_prompt.shape
    dec_b, dec_s, _ = x_sample.shape
    n_pool = cache_k.shape[1]
    n_pages = page_table.shape[1]
    past = n_pages * PAGE_SIZE
    tp = batch * seq
    ts = dec_b * dec_s
    nb = seq // MOBA_BLOCK

    wqkv = w_qkv.astype(_BF16)
    wo = w_o.astype(_BF16)
    wup = _chunk_cols(w_up).astype(_BF16)
    cw = _chunk_cols(conv_w)
    cb = _chunk_cols(conv_b[:, None, :])
    wdn = w_down.astype(_BF16)
    g_attn2 = g_attn[:, None, :]
    g_ffn2 = g_ffn[:, None, :]
    g_fin2 = g_final[None, :]

    hidx = jnp.arange(1, N_HEADS + 1, dtype=_F32)
    slopes = jnp.exp2(-8.0 * hidx / N_HEADS)
    assert nb <= SLOPE_LANE0 - FLAG_LANE0 and past % MOBA_BLOCK == 0 and dec_s <= PAGE_SIZE
    pieces = jnp.stack(_bf16_pieces(slopes * LOG2E) * 2, axis=1)
    stab = jnp.zeros((N_HEADS, 8, SLAB), _F32).at[:, :, SLOPE_LANE0:SLOPE_LANE0 + 6].set(pieces[:, None, :])
    slope_rows = jnp.repeat(slopes, dec_s)[:, None]
    tok_rows = jnp.tile(jnp.arange(dec_s, dtype=_F32), N_HEADS)[:, None]
    qpos_rows = tok_rows + float(past)
    tok_seq = jnp.tile(jnp.arange(dec_s, dtype=_F32), dec_b)[:, None]

    cache_kt = cache_k.transpose(0, 1, 3, 4, 2).reshape(depth, n_pool, KV_DIM, PAGE_SIZE)
    cache_vt = cache_v.transpose(0, 1, 3, 4, 2).reshape(depth, n_pool, KV_DIM, PAGE_SIZE)

    st = state_conv
    zero = jnp.zeros_like(st[:, :, :1])
    s1 = jnp.concatenate([st[:, :, 1:2], zero, zero, zero][:dec_s], axis=2)
    s2 = jnp.concatenate([st[:, :, 0:1], st[:, :, 1:2], zero, zero][:dec_s], axis=2)
    s1 = _chunk_cols(s1.reshape(depth, ts, 2 * D_FF))
    s2 = _chunk_cols(s2.reshape(depth, ts, 2 * D_FF))

    xp = x_prompt.reshape(tp, D_MODEL)
    xs = x_sample.reshape(ts, D_MODEL)
    kp_l, vp_l, ks_l, vs_l, cp_l, cs_l = [], [], [], [], [], []
    yp = ys = None
    pad_own = jnp.zeros((dec_b, PAGE_SIZE - dec_s, KV_DIM), _BF16)
    for l in range(depth):
        last = l == depth - 1
        gfin = g_fin2 if last else None
        moba = l % 2 == 0
        q, k, v, kslab, vslab, *km = _rms_qkv(xp, g_attn2[l], wqkv[l], tm=512, with_kmean=moba)
        if moba:
            kmt = km[0].reshape(batch, nb, N_KV_HEADS, HEAD_DIM).transpose(0, 2, 1, 3)
            kmt = jnp.pad(kmt, ((0, 0), (0, 0), (0, SLOPE_LANE0 - FLAG_LANE0 - nb),
                                (0, SLAB - HEAD_DIM)))
            knorm = jnp.sqrt(jnp.max(km[1].reshape(batch, nb, N_KV_HEADS, HEAD_DIM), axis=(1, 3)))
            o = _moba_prompt(knorm.reshape(batch * N_KV_HEADS), stab, q, kslab, vslab, kmt,
                             batch=batch, seq=seq)
        else:
            o = _stick_prompt(q, kslab, vslab, batch=batch, seq=seq)
        outs = _oproj_ffn(xp, o, wo[l], g_ffn2[l], wup[l], cw[l], cb[l], wdn[l],
                          tm=512, seq=seq, g_final=gfin)
        xp, cst = outs[0], outs[1]
        if last:
            yp = outs[2]
        kp_l.append(k.reshape(batch, seq, N_KV_HEADS, HEAD_DIM))
        vp_l.append(v.reshape(batch, seq, N_KV_HEADS, HEAD_DIM))
        cp_l.append(_unchunk_cols(cst[:, :, 6:8, :]))
        qs, ksn, vsn, _, _ = _rms_qkv(xs, g_attn2[l], wqkv[l], tm=ts, with_kmean=False)
        qnat = qs[:, :, :HEAD_DIM].reshape(N_KV_HEADS, GROUP, dec_b, dec_s, HEAD_DIM)
        q_rows = qnat.transpose(2, 1, 3, 0, 4).reshape(dec_b, GROUP * dec_s, KV_DIM)
        k_own = jnp.concatenate([ksn.astype(_BF16).reshape(dec_b, dec_s, KV_DIM), pad_own], axis=1)
        v_own = jnp.concatenate([vsn.astype(_BF16).reshape(dec_b, dec_s, KV_DIM), pad_own], axis=1)
        if moba:
            o_rows = _moba_sample(page_table, slope_rows, qpos_rows, q_rows, k_own, v_own,
                                  cache_kt, cache_vt, layer=l)
        else:
            o_rows = _stick_sample(page_table, tok_rows, q_rows, k_own, v_own, cache_kt, cache_vt, layer=l)
        os_ = o_rows.reshape(dec_b, GROUP, dec_s, N_KV_HEADS, HEAD_DIM).transpose(0, 2, 3, 1, 4)
        outs = _oproj_ffn(xs, os_.reshape(ts, D_MODEL), wo[l], g_ffn2[l], wup[l], cw[l], cb[l], wdn[l],
                          tm=ts, seq=dec_s, sample_state=(s1[l], s2[l], tok_seq), g_final=gfin)
        xs, u_all = outs[0], outs[1]
        if last:
            ys = outs[2]
        ks_l.append(ksn.reshape(dec_b, dec_s, N_KV_HEADS, HEAD_DIM))
        vs_l.append(vsn.reshape(dec_b, dec_s, N_KV_HEADS, HEAD_DIM))
        u_full = _unchunk_cols(u_all).reshape(dec_b, dec_s, 2 * D_FF)
        cs_l.append(u_full[:, dec_s - (CONV_W - 1):, :])
    return (yp.reshape(batch, seq, D_MODEL), ys.reshape(dec_b, dec_s, D_MODEL),
            jnp.stack(kp_l), jnp.stack(vp_l), jnp.stack(ks_l), jnp.stack(vs_l),
            jnp.stack(cp_l), jnp.stack(cs_l))
```

```python
import functools

import jax
import jax.numpy as jnp
from jax import lax
from jax.experimental import pallas as pl
from jax.experimental.pallas import tpu as pltpu

D_MODEL = 1024
N_HEADS = 16
HEAD_DIM = 64
N_KV_HEADS = 4
GROUP = N_HEADS // N_KV_HEADS
KV_DIM = N_KV_HEADS * HEAD_DIM
SLAB = 2 * HEAD_DIM
MOBA_BLOCK = 256
MOBA_TOPK = 3
D_FF = 2816
CONV_W = 3
PAGE_SIZE = 128
RMS_EPS = 1e-6
NEG_INF = -1e30
SCALE = HEAD_DIM ** -0.5

FF_CHUNK = 256
N_FF_CHUNKS = D_FF // FF_CHUNK
STICK_KB = 128
LOG2E = 1.4426950408889634
STICK_DEAD = -110.0 * LOG2E
MOBA_DEAD = 152.0
NORM_SLACK = 1.04
GATE_LANES = 128
FLAG_LANE0 = HEAD_DIM
SLOPE_LANE0 = 96
PAGES_PER_STEP = 8
MOBA_PAGES_PER_STEP = 16
VMEM_LIMIT = 56 * 1024 * 1024
TQ = 256
STICK_TQ = 256

_BF16 = jnp.bfloat16
_F32 = jnp.float32


def _nt_dot(a, b):
    return lax.dot_general(a, b, (((1,), (1,)), ((), ())), preferred_element_type=_F32)


def _lane_group(shape):
    return lax.broadcasted_iota(jnp.int32, shape, len(shape) - 1) >> 6


def _qkv_body(*refs, tm, stacked):
    if stacked:
        x_ref, g_ref, w_ref, _, _, q_ref, k_ref, v_ref, ks_ref, vs_ref, *km_ref = refs
    else:
        x_ref, g_ref, w_ref, q_ref, k_ref, v_ref, ks_ref, vs_ref, *km_ref = refs
    x = x_ref[...]
    ms = jnp.mean(x * x, axis=-1, keepdims=True)
    h = (x * lax.rsqrt(ms + RMS_EPS)) * g_ref[...]
    qkv = jnp.dot(h.astype(_BF16), w_ref[...], preferred_element_type=_F32)
    for j in range(N_HEADS // 2):
        pair = qkv[:, j * SLAB:(j + 1) * SLAB]
        q_ref[2 * j] = pair
        q_ref[2 * j + 1] = pltpu.roll(pair, HEAD_DIM, 1)
    k = qkv[:, D_MODEL:D_MODEL + KV_DIM]
    v = qkv[:, D_MODEL + KV_DIM:D_MODEL + 2 * KV_DIM]
    k_ref[...] = k.T if stacked else k
    v_ref[...] = v.T if stacked else v
    for j in range(N_KV_HEADS // 2):
        for src, dst in ((k, ks_ref), (v, vs_ref)):
            pair = src[:, j * SLAB:(j + 1) * SLAB]
            dst[2 * j] = pair.astype(_BF16)
            dst[2 * j + 1] = pltpu.roll(pair, HEAD_DIM, 1).astype(_BF16)
    if km_ref:
        li = lax.broadcasted_iota(jnp.int32, (KV_DIM, KV_DIM), 0) >> 6
        lj = lax.broadcasted_iota(jnp.int32, (KV_DIM, KV_DIM), 1) >> 6
        k2 = jnp.dot((k * k).astype(_BF16), jnp.where(li == lj, 1.0, 0.0).astype(_BF16),
                     preferred_element_type=_F32)
        for j in range(tm // MOBA_BLOCK):
            blk = slice(j * MOBA_BLOCK, (j + 1) * MOBA_BLOCK)
            km_ref[0][j] = jnp.mean(k[blk], axis=0, keepdims=True)
            km_ref[1][j] = jnp.max(k2[blk], axis=0, keepdims=True)


def _rms_qkv(x, g, w, *, tm, with_kmean, kv_stack=None, layer=0):
    t = x.shape[0]
    stacked = kv_stack is not None
    if stacked:
        seq = kv_stack[0].shape[3]
        tps = seq // tm
        kv_shape = [jax.ShapeDtypeStruct(kv_stack[0].shape, _F32)] * 2
        row = pl.BlockSpec((None, None, KV_DIM, tm), lambda i: (layer, i // tps, 0, i % tps))
    else:
        kv_shape = [jax.ShapeDtypeStruct((t, KV_DIM), _F32)] * 2
        row = pl.BlockSpec((tm, KV_DIM), lambda i: (i, 0))
    out_shape = [jax.ShapeDtypeStruct((N_HEADS, t, SLAB), _F32)] + kv_shape + [
        jax.ShapeDtypeStruct((N_KV_HEADS, t, SLAB), _BF16),
        jax.ShapeDtypeStruct((N_KV_HEADS, t, SLAB), _BF16),
    ]
    slab = pl.BlockSpec((N_KV_HEADS, tm, SLAB), lambda i: (0, i, 0))
    out_specs = [pl.BlockSpec((N_HEADS, tm, SLAB), lambda i: (0, i, 0)), row, row, slab, slab]
    if with_kmean:
        for _ in range(2):
            out_shape.append(jax.ShapeDtypeStruct((t // MOBA_BLOCK, 1, KV_DIM), _F32))
            out_specs.append(pl.BlockSpec((tm // MOBA_BLOCK, 1, KV_DIM), lambda i: (i, 0, 0)))
    in_specs = [
        pl.BlockSpec((tm, D_MODEL), lambda i: (i, 0)),
        pl.BlockSpec((1, D_MODEL), lambda i: (0, 0)),
        pl.BlockSpec(w.shape, lambda i: (0, 0)),
    ]
    args = [x, g, w]
    aliases = {}
    if stacked:
        in_specs += [pl.BlockSpec(memory_space=pl.ANY)] * 2
        args += list(kv_stack)
        aliases = {3: 1, 4: 2}
    return pl.pallas_call(
        functools.partial(_qkv_body, tm=tm, stacked=stacked),
        grid=(t // tm,),
        in_specs=in_specs,
        out_specs=out_specs,
        out_shape=out_shape,
        input_output_aliases=aliases,
        compiler_params=pltpu.CompilerParams(dimension_semantics=("arbitrary",),
                                             vmem_limit_bytes=VMEM_LIMIT),
        name="rms_qkv",
    )(*args)


def _top3_select(gate, eligible, axis):
    idx = lax.broadcasted_iota(jnp.int32, gate.shape, axis).astype(_F32)
    neg = jnp.float32(-jnp.inf)
    gm = jnp.where(eligible, gate, neg)
    sel = jnp.zeros(gate.shape, _F32)
    for _ in range(MOBA_TOPK):
        mx = jnp.max(gm, axis=axis, keepdims=True)
        first = jnp.min(jnp.where(gm == mx, idx, jnp.float32(1e9)), axis=axis, keepdims=True)
        pick = jnp.logical_and(idx == first, mx > neg)
        sel = jnp.where(pick, 1.0, sel)
        gm = jnp.where(pick, neg, gm)
    return sel


def _stick_tri():
    j = lax.broadcasted_iota(jnp.int32, (2 * STICK_KB, 2 * STICK_KB), 0) & (STICK_KB - 1)
    s = lax.broadcasted_iota(jnp.int32, (2 * STICK_KB, 2 * STICK_KB), 1)
    return jnp.where(jnp.logical_or(s >= STICK_KB, j > s), 1.0, 0.0).astype(_BF16)


def _stick_chunk(z, causal, carry, tri, vc_nt=None, vc=None):
    nz = -z
    e = jnp.exp2(jnp.minimum(z, nz))
    log_keep = jnp.minimum(nz, 0.0) - jnp.log2(1.0 + e)
    log_sig = z + log_keep
    if causal is not None:
        log_keep = jnp.where(causal, log_keep, 0.0)
    hi = log_keep.astype(_BF16)
    lo = (log_keep - hi.astype(_F32)).astype(_BF16)
    ct = jnp.dot(jnp.concatenate([hi, lo], axis=1), tri, preferred_element_type=_F32)
    log_after = carry + ct[:, :STICK_KB]
    a = jnp.exp2(log_sig + log_after)
    if causal is not None:
        a = jnp.where(causal, a, 0.0)
    a = a.astype(_BF16)
    pv = _nt_dot(a, vc_nt) if vc is None else jnp.dot(a, vc, preferred_element_type=_F32)
    return pv, carry + ct[:, STICK_KB:]


def _pair_heads(acc_ref, o_ref, normalise):
    low = lax.broadcasted_iota(jnp.int32, acc_ref.shape[1:], 1) < HEAD_DIM
    for j in range(N_HEADS // 2):
        a0 = acc_ref[2 * j]
        a1 = acc_ref[2 * j + 1]
        out = jnp.where(low, a0, pltpu.roll(a1, HEAD_DIM, 1))
        if normalise:
            out = out / jnp.where(low, pltpu.roll(a0, HEAD_DIM, 1), a1)
        o_ref[:, j * SLAB:(j + 1) * SLAB] = out.astype(o_ref.dtype)


def _moba_prompt_body(kmax_ref, stab_ref, q_ref, k_ref, v_ref, km_ref, o_ref, qa_ref, m_ref, acc_ref, *, tq):
    i = pl.program_id(1)
    rows = GROUP * tq
    low = lax.broadcasted_iota(jnp.int32, (tq, SLAB), 1) < HEAD_DIM
    n_flags = SLOPE_LANE0 - FLAG_LANE0
    eligible = lax.broadcasted_iota(jnp.int32, (n_flags, tq), 0) < i
    krow = lax.broadcasted_iota(jnp.int32, (MOBA_BLOCK, SLAB), 0)
    klane = lax.broadcasted_iota(jnp.int32, (MOBA_BLOCK, SLAB), 1)
    klow = klane < HEAD_DIM
    one = jnp.ones((MOBA_BLOCK, SLAB), _BF16)

    def key_aux(block_offset, c):
        f = jnp.where(klane == FLAG_LANE0 + c, NEG_INF, 0.0)
        f = jnp.where(jnp.logical_and(klane >= SLOPE_LANE0, klane < SLOPE_LANE0 + 3), block_offset, f)
        f = jnp.where(jnp.logical_and(klane >= SLOPE_LANE0 + 3, klane < SLOPE_LANE0 + 6),
                      krow.astype(_F32), f)
        return f.astype(_BF16)

    def kv_head(n, _):
        km = km_ref[n]
        for g in range(GROUP):
            h = n * GROUP + g
            qh = q_ref[h]
            gate_t = lax.dot_general(km, qh, (((1,), (1,)), ((), ())),
                                     precision=lax.Precision.HIGHEST, preferred_element_type=_F32)
            sel_t = _top3_select(gate_t, eligible, 0)
            notsel_t = jnp.where(eligible, 1.0 - sel_t, 0.0)
            notsel = jnp.concatenate([jnp.zeros((FLAG_LANE0, tq), _F32), notsel_t,
                                      jnp.zeros((GATE_LANES - FLAG_LANE0 - n_flags, tq), _F32)],
                                     axis=0).T
            q2 = qh * (SCALE * LOG2E)
            qa_ref[h] = jnp.where(low, q2, notsel + stab_ref[h][0:1, :]).astype(_BF16)
            qn = jnp.sqrt(jnp.sum(jnp.where(low, q2 * q2, 0.0), axis=1, keepdims=True))
            m_ref[h] = jnp.broadcast_to(qn, (tq, SLAB))
        return 0

    lax.fori_loop(0, N_KV_HEADS, kv_head, 0)

    off = pl.multiple_of(i * MOBA_BLOCK, MOBA_BLOCK)
    f_own = key_aux(0.0, i)
    qrow = lax.broadcasted_iota(jnp.int32, (rows, MOBA_BLOCK), 0) & (tq - 1)
    kcol = lax.broadcasted_iota(jnp.int32, (rows, MOBA_BLOCK), 1)
    causal = qrow >= kcol
    keep = []
    for n in range(N_KV_HEADS):
        hs = pl.ds(n * GROUP, GROUP)
        kk = jnp.where(klow, k_ref[n, pl.ds(off, MOBA_BLOCK), :], f_own)
        vv = jnp.where(klow, v_ref[n, pl.ds(off, MOBA_BLOCK), :], one)
        s = _nt_dot(qa_ref[hs].reshape(rows, SLAB), kk)
        s = jnp.where(causal, s, NEG_INF)
        m0 = jnp.max(s, axis=1, keepdims=True)
        p = jnp.exp2(s - m0)
        qn = m_ref[hs].reshape(rows, SLAB)[:, 0:1]
        m_ref[hs] = jnp.broadcast_to(m0, (rows, SLAB)).reshape(GROUP, tq, SLAB)
        acc_ref[hs] = jnp.dot(p.astype(_BF16), vv, preferred_element_type=_F32).reshape(GROUP, tq, SLAB)
        slope_min = 2.0 ** (-8.0 * (n * GROUP + GROUP) / N_HEADS) * LOG2E * 0.999
        reach = (qn * (kmax_ref[pl.program_id(0) * N_KV_HEADS + n] * NORM_SLACK) + MOBA_DEAD - m0)
        blocks = jnp.max(reach) * (1.0 / (MOBA_BLOCK * slope_min)) - 1.0 / MOBA_BLOCK
        k_n = jnp.int32(0)
        for d in range(n_flags):
            k_n = k_n + (blocks >= float(d)).astype(jnp.int32)
        keep.append(k_n if n == 0 else jnp.maximum(k_n, keep[-1]))

    def past(c, _, groups):
        offc = pl.multiple_of(c * MOBA_BLOCK, MOBA_BLOCK)
        f_c = key_aux(((c - i) * MOBA_BLOCK).astype(_F32), c)
        for n in groups:
            hs = pl.ds(n * GROUP, GROUP)
            kk = jnp.where(klow, k_ref[n, pl.ds(offc, MOBA_BLOCK), :], f_c)
            vv = jnp.where(klow, v_ref[n, pl.ds(offc, MOBA_BLOCK), :], one)
            s = _nt_dot(qa_ref[hs].reshape(rows, SLAB), kk)
            m_old = m_ref[hs].reshape(rows, SLAB)
            m_new = jnp.maximum(m_old, jnp.max(s, axis=1, keepdims=True))
            alpha = jnp.exp2(m_old - m_new)
            p = jnp.exp2(s - jnp.concatenate([m_new, m_new], axis=1))
            pv = jnp.dot(p.astype(_BF16), vv, preferred_element_type=_F32)
            m_ref[hs] = m_new.reshape(GROUP, tq, SLAB)
            acc_ref[hs] = (alpha * acc_ref[hs].reshape(rows, SLAB) + pv).reshape(GROUP, tq, SLAB)
        return 0

    start = [jnp.maximum(i - k_n, 0) for k_n in keep]
    for n in range(N_KV_HEADS - 1, -1, -1):
        stop = start[n - 1] if n > 0 else i
        body = functools.partial(past, groups=tuple(range(n, N_KV_HEADS)))
        if n == 0:
            lax.fori_loop(start[n], stop, body, 0)
            continue
        pairs = (stop - start[n]) // 2

        def two(t, _, body=body, first=start[n]):
            body(first + 2 * t, 0)
            body(first + 2 * t + 1, 0)
            return 0

        lax.fori_loop(0, pairs, two, 0)
        lax.fori_loop(start[n] + 2 * pairs, stop, body, 0)
    _pair_heads(acc_ref, o_ref, normalise=True)


def _moba_prompt(kmax, stab, q, ks, vs, kmtab, *, batch, seq):
    tq = TQ
    nq = seq // tq
    return pl.pallas_call(
        functools.partial(_moba_prompt_body, tq=tq),
        grid=(batch, nq),
        in_specs=[
            pl.BlockSpec(memory_space=pltpu.SMEM),
            pl.BlockSpec(stab.shape, lambda b, i: (0, 0, 0)),
            pl.BlockSpec((N_HEADS, tq, SLAB), lambda b, i: (0, b * nq + i, 0)),
            pl.BlockSpec((N_KV_HEADS, seq, SLAB), lambda b, i: (0, b, 0)),
            pl.BlockSpec((N_KV_HEADS, seq, SLAB), lambda b, i: (0, b, 0)),
            pl.BlockSpec((None,) + kmtab.shape[1:], lambda b, i: (b, 0, 0, 0)),
        ],
        out_specs=pl.BlockSpec((tq, D_MODEL), lambda b, i: (b * nq + i, 0)),
        out_shape=jax.ShapeDtypeStruct((batch * seq, D_MODEL), _BF16),
        scratch_shapes=[pltpu.VMEM((N_HEADS, tq, SLAB), _BF16),
                        pltpu.VMEM((N_HEADS, tq, SLAB), _F32),
                        pltpu.VMEM((N_HEADS, tq, SLAB), _F32)],
        compiler_params=pltpu.CompilerParams(dimension_semantics=("arbitrary", "arbitrary"),
                                             vmem_limit_bytes=VMEM_LIMIT),
        name="moba_prompt",
    )(kmax, stab, q, ks, vs, kmtab)


def _stick_prompt_body(q_ref, k_ref, v_ref, o_ref, qa_ref, acc_ref, carry_ref, *, tq):
    i = pl.program_id(1)
    rows = GROUP * tq
    low = lax.broadcasted_iota(jnp.int32, q_ref.shape, 2) < HEAD_DIM
    qa_ref[...] = jnp.where(low, q_ref[...] * (SCALE * LOG2E), 0.0).astype(_BF16)
    acc_ref[...] = jnp.zeros(acc_ref.shape, _F32)
    carry_ref[...] = jnp.zeros(carry_ref.shape, _F32)
    tri = _stick_tri()
    qrow = lax.broadcasted_iota(jnp.int32, (rows, STICK_KB), 0) & (tq - 1)
    kcol = lax.broadcasted_iota(jnp.int32, (rows, STICK_KB), 1)
    rel = qrow - kcol
    chunks_per_tile = tq // STICK_KB

    def chunk(c, masked):
        off = pl.multiple_of(c * STICK_KB, STICK_KB)
        causal = (rel + (i * tq - c * STICK_KB) > 0) if masked else None
        top = jnp.float32(-jnp.inf)
        for n in range(N_KV_HEADS):
            hs = pl.ds(n * GROUP, GROUP)
            z = _nt_dot(qa_ref[hs].reshape(rows, SLAB), k_ref[n, pl.ds(off, STICK_KB), :])
            pv, carry = _stick_chunk(z, causal, carry_ref[hs].reshape(rows, STICK_KB), tri,
                                     vc=v_ref[n, pl.ds(off, STICK_KB), :])
            acc_ref[hs] += pv.reshape(GROUP, tq, SLAB)
            carry_ref[hs] = carry.reshape(GROUP, tq, STICK_KB)
            top = jnp.maximum(top, jnp.max(carry))
        return top

    top = jnp.float32(0.0)
    for d in range(chunks_per_tile):
        top = chunk((i + 1) * chunks_per_tile - 1 - d, True)

    def cond(state):
        c, alive = state
        return jnp.logical_and(c >= 0, alive > 0)

    def step(state):
        c, _ = state
        return c - 1, (chunk(c, False) > STICK_DEAD).astype(jnp.int32)

    lax.while_loop(cond, step, (i * chunks_per_tile - 1, (top > STICK_DEAD).astype(jnp.int32)))
    _pair_heads(acc_ref, o_ref, normalise=False)


def _stick_prompt(q, ks, vs, *, batch, seq):
    tq = STICK_TQ
    nq = seq // tq
    return pl.pallas_call(
        functools.partial(_stick_prompt_body, tq=tq),
        grid=(batch, nq),
        in_specs=[
            pl.BlockSpec((N_HEADS, tq, SLAB), lambda b, i: (0, b * nq + i, 0)),
            pl.BlockSpec((N_KV_HEADS, seq, SLAB), lambda b, i: (0, b, 0)),
            pl.BlockSpec((N_KV_HEADS, seq, SLAB), lambda b, i: (0, b, 0)),
        ],
        out_specs=pl.BlockSpec((tq, D_MODEL), lambda b, i: (b * nq + i, 0)),
        out_shape=jax.ShapeDtypeStruct((batch * seq, D_MODEL), _BF16),
        scratch_shapes=[pltpu.VMEM((N_HEADS, tq, SLAB), _BF16),
                        pltpu.VMEM((N_HEADS, tq, SLAB), _F32),
                        pltpu.VMEM((N_HEADS, tq, STICK_KB), _F32)],
        compiler_params=pltpu.CompilerParams(dimension_semantics=("arbitrary", "arbitrary"),
                                             vmem_limit_bytes=VMEM_LIMIT),
        name="stick_prompt",
    )(q, ks, vs)


def _sample_rows(q_ref, scale):
    q = q_ref[0] * scale
    grp = _lane_group(q.shape)
    return jnp.concatenate([jnp.where(grp == n, q, 0.0) for n in range(N_KV_HEADS)], axis=0)


def _sample_out(acc):
    r = acc.shape[0] // N_KV_HEADS
    grp = _lane_group((r, KV_DIM))
    out = jnp.zeros((r, KV_DIM), _F32)
    for n in range(N_KV_HEADS):
        out = jnp.where(grp == n, acc[n * r:(n + 1) * r], out)
    return out


def _moba_sample_body(pt_ref, slope_ref, qpos_ref, q_ref, kown_ref, vown_ref, *rest, n_steps, pps):
    kp = rest[:pps]
    vp = rest[pps:2 * pps]
    o_ref, oall_ref, mtab_ref, ltab_ref, kmt_ref = rest[2 * pps:]
    sp = pl.program_id(1)
    qm = _sample_rows(q_ref, SCALE)
    qb = qm.astype(_BF16)
    rows = qm.shape[0]
    slope = slope_ref[...]
    qpos = qpos_ref[...]
    gcol = lax.broadcasted_iota(jnp.int32, (rows, GATE_LANES), 1)
    kmcol = lax.broadcasted_iota(jnp.int32, (KV_DIM, GATE_LANES), 1)
    kcol = lax.broadcasted_iota(jnp.int32, (rows, MOBA_BLOCK), 1).astype(_F32)
    pages_per_block = MOBA_BLOCK // PAGE_SIZE
    blocks_per_step = pps // pages_per_block

    @pl.when(sp == 0)
    def _():
        for tab in (mtab_ref, ltab_ref, kmt_ref):
            tab[...] = jnp.zeros(tab.shape, _F32)

    kmt, mtab, ltab = kmt_ref[...], mtab_ref[...], ltab_ref[...]
    for jb in range(blocks_per_step):
        m = sp * blocks_per_step + jb
        kblk = jnp.concatenate([kp[jb * pages_per_block + u][...] for u in range(pages_per_block)], axis=1)
        vblk = jnp.concatenate([vp[jb * pages_per_block + u][...] for u in range(pages_per_block)], axis=1)
        ksum = jnp.sum(kblk, axis=1, keepdims=True)
        kmt = jnp.where(kmcol == m, ksum * (1.0 / MOBA_BLOCK), kmt)
        kpos = kcol + (m * MOBA_BLOCK).astype(_F32)
        s = jnp.dot(qb, kblk.astype(_BF16), preferred_element_type=_F32) - slope * (qpos - kpos)
        mb = jnp.max(s, axis=1, keepdims=True)
        p = jnp.exp(s - mb)
        oall_ref[m] = _nt_dot(p.astype(_BF16), vblk.astype(_BF16))
        here = gcol == m
        mtab = jnp.where(here, mb, mtab)
        ltab = jnp.where(here, jnp.sum(p, axis=1, keepdims=True), ltab)
    kmt_ref[...] = kmt
    mtab_ref[...] = mtab
    ltab_ref[...] = ltab

    @pl.when(sp == n_steps - 1)
    def _():
        n_blocks = n_steps * blocks_per_step
        ocol = lax.broadcasted_iota(jnp.int32, (rows, PAGE_SIZE), 1).astype(_F32)
        dist = (qpos - (n_blocks * MOBA_BLOCK)) - ocol
        s = _nt_dot(qb, kown_ref[0]) - slope * dist
        s = jnp.where(dist >= 0.0, s, NEG_INF)
        m_own = jnp.max(s, axis=1, keepdims=True)
        p = jnp.exp(s - m_own)
        l_own = jnp.sum(p, axis=1, keepdims=True)
        o_own = jnp.dot(p.astype(_BF16), vown_ref[0], preferred_element_type=_F32)

        gate = jnp.dot(qm, kmt_ref[...], precision=lax.Precision.HIGHEST, preferred_element_type=_F32)
        picked = _top3_select(gate, gcol < n_blocks, 1) > 0.0
        mtab = mtab_ref[...]
        m_all = jnp.maximum(m_own, jnp.max(jnp.where(picked, mtab, NEG_INF), axis=1, keepdims=True))
        w = jnp.where(picked, jnp.exp(mtab - m_all), 0.0)
        w_own = jnp.exp(m_own - m_all)
        l_all = w_own * l_own + jnp.sum(w * ltab_ref[...], axis=1, keepdims=True)

        def merge(mi, acc):
            wm = jnp.sum(jnp.where(gcol == mi, w, 0.0), axis=1, keepdims=True)
            return acc + wm * oall_ref[mi]

        acc = lax.fori_loop(0, n_blocks, merge, w_own * o_own)
        o_ref[0] = _sample_out(acc / l_all).astype(o_ref.dtype)


def _page_specs(layer, page_of, pps=PAGES_PER_STEP):
    def spec(j):
        return pl.BlockSpec((None, None, KV_DIM, PAGE_SIZE),
                            lambda b, s, pt, *_: (layer, pt[b, page_of(s, j)], 0, 0))
    return [spec(j) for j in range(pps)] * 2


def _moba_sample(page_table, slope_rows, qpos_rows, q_rows, k_own, v_own, cache_kt, cache_vt, *, layer):
    dec_b, n_pages = page_table.shape
    pps = MOBA_PAGES_PER_STEP if n_pages % MOBA_PAGES_PER_STEP == 0 else PAGES_PER_STEP
    n_steps = n_pages // pps
    rows = q_rows.shape[1] * N_KV_HEADS
    n_blocks = n_pages * PAGE_SIZE // MOBA_BLOCK
    grid_spec = pltpu.PrefetchScalarGridSpec(
        num_scalar_prefetch=1,
        grid=(dec_b, n_steps),
        in_specs=[
            pl.BlockSpec((rows, 1), lambda b, s, pt: (0, 0)),
            pl.BlockSpec((rows, 1), lambda b, s, pt: (0, 0)),
            pl.BlockSpec((1, q_rows.shape[1], KV_DIM), lambda b, s, pt: (b, 0, 0)),
            pl.BlockSpec((1, PAGE_SIZE, KV_DIM), lambda b, s, pt: (b, 0, 0)),
            pl.BlockSpec((1, PAGE_SIZE, KV_DIM), lambda b, s, pt: (b, 0, 0)),
        ] + _page_specs(layer, lambda s, j: s * pps + j, pps),
        out_specs=pl.BlockSpec((1, q_rows.shape[1], KV_DIM), lambda b, s, pt: (b, 0, 0)),
        scratch_shapes=[pltpu.VMEM((n_blocks, rows, KV_DIM), _F32),
                        pltpu.VMEM((rows, GATE_LANES), _F32),
                        pltpu.VMEM((rows, GATE_LANES), _F32),
                        pltpu.VMEM((KV_DIM, GATE_LANES), _F32)],
    )
    return pl.pallas_call(
        functools.partial(_moba_sample_body, n_steps=n_steps, pps=pps),
        grid_spec=grid_spec,
        out_shape=jax.ShapeDtypeStruct(q_rows.shape, _BF16),
        compiler_params=pltpu.CompilerParams(dimension_semantics=("arbitrary", "arbitrary"),
                                             vmem_limit_bytes=VMEM_LIMIT),
        name="moba_sample",
    )(page_table, slope_rows, qpos_rows, q_rows, k_own, v_own,
      *([cache_kt] * pps), *([cache_vt] * pps))


def _stick_pages(qb, kp, vp, tri, acc_ref, carry_ref, alive_ref):
    for j in range(PAGES_PER_STEP):
        @pl.when(alive_ref[0] > 0)
        def _():
            z = jnp.dot(qb, kp[j][...].astype(_BF16), preferred_element_type=_F32)
            pv, carry = _stick_chunk(z, None, carry_ref[...], tri, vc_nt=vp[j][...].astype(_BF16))
            acc_ref[...] += pv
            carry_ref[...] = carry
            alive_ref[0] = (jnp.max(carry) > STICK_DEAD).astype(jnp.int32)


def _stick_sample_first_body(pt_ref, tok_ref, q_ref, kown_ref, vown_ref, *rest):
    kp = rest[:PAGES_PER_STEP]
    vp = rest[PAGES_PER_STEP:2 * PAGES_PER_STEP]
    o_ref, acc_ref, carry_ref, alive_ref = rest[2 * PAGES_PER_STEP:]
    qb = _sample_rows(q_ref, SCALE * LOG2E).astype(_BF16)
    rows = qb.shape[0]
    tri = _stick_tri()
    kcol = lax.broadcasted_iota(jnp.int32, (rows, STICK_KB), 1).astype(_F32)
    causal = kcol < tok_ref[...]
    z = _nt_dot(qb, kown_ref[0])
    pv, carry = _stick_chunk(z, causal, jnp.zeros((rows, STICK_KB), _F32), tri, vc=vown_ref[0])
    acc_ref[0] = pv
    carry_ref[0] = carry
    alive_ref[0] = (jnp.max(carry) > STICK_DEAD).astype(jnp.int32)
    _stick_pages(qb, kp, vp, tri, acc_ref.at[0], carry_ref.at[0], alive_ref)
    o_ref[0] = _sample_out(acc_ref[0]).astype(o_ref.dtype)


def _stick_sample_rest_body(pt_ref, live_ref, q_ref, acc_in_ref, carry_in_ref, *rest, n_steps):
    kp = rest[:PAGES_PER_STEP]
    vp = rest[PAGES_PER_STEP:2 * PAGES_PER_STEP]
    o_ref, acc_ref, carry_ref, alive_ref = rest[2 * PAGES_PER_STEP:]
    sp = pl.program_id(1)
    qb = _sample_rows(q_ref, SCALE * LOG2E).astype(_BF16)

    @pl.when(sp == 0)
    def _():
        acc_ref[...] = acc_in_ref[0]
        carry_ref[...] = carry_in_ref[0]
        alive_ref[0] = live_ref[pl.program_id(0)]

    _stick_pages(qb, kp, vp, _stick_tri(), acc_ref, carry_ref, alive_ref)

    @pl.when(sp == n_steps - 1)
    def _():
        o_ref[0] = _sample_out(acc_ref[...]).astype(o_ref.dtype)


def _stick_sample(page_table, tok_rows, q_rows, k_own, v_own, cache_kt, cache_vt, *, layer):
    dec_b, n_pages = page_table.shape
    n_steps = n_pages // PAGES_PER_STEP
    qr = q_rows.shape[1]
    rows = qr * N_KV_HEADS
    per_b = lambda b, s, *_: (b, 0, 0)
    params = pltpu.CompilerParams(dimension_semantics=("arbitrary", "arbitrary"),
                                  vmem_limit_bytes=VMEM_LIMIT)
    pages = [cache_kt] * PAGES_PER_STEP + [cache_vt] * PAGES_PER_STEP
    o_rows, acc, carry = pl.pallas_call(
        _stick_sample_first_body,
        grid_spec=pltpu.PrefetchScalarGridSpec(
            num_scalar_prefetch=1,
            grid=(dec_b, 1),
            in_specs=[
                pl.BlockSpec((rows, 1), lambda b, s, *_: (0, 0)),
                pl.BlockSpec((1, qr, KV_DIM), per_b),
                pl.BlockSpec((1, PAGE_SIZE, KV_DIM), per_b),
                pl.BlockSpec((1, PAGE_SIZE, KV_DIM), per_b),
            ] + _page_specs(layer, lambda s, j: n_pages - 1 - j),
            out_specs=[pl.BlockSpec((1, qr, KV_DIM), per_b),
                       pl.BlockSpec((1, rows, KV_DIM), per_b),
                       pl.BlockSpec((1, rows, STICK_KB), per_b)],
            scratch_shapes=[pltpu.SMEM((1,), jnp.int32)],
        ),
        out_shape=[jax.ShapeDtypeStruct(q_rows.shape, _BF16),
                   jax.ShapeDtypeStruct((dec_b, rows, KV_DIM), _F32),
                   jax.ShapeDtypeStruct((dec_b, rows, STICK_KB), _F32)],
        compiler_params=params,
        name="stick_sample_first",
    )(page_table, tok_rows, q_rows, k_own, v_own, *pages)
    if n_steps == 1:
        return o_rows
    live = (jnp.max(carry, axis=(1, 2)) > STICK_DEAD).astype(jnp.int32)
    pt_rest = jnp.where(live[:, None] > 0, page_table, page_table[0, 0])
    rest = pl.pallas_call(
        functools.partial(_stick_sample_rest_body, n_steps=n_steps - 1),
        grid_spec=pltpu.PrefetchScalarGridSpec(
            num_scalar_prefetch=2,
            grid=(dec_b, n_steps - 1),
            in_specs=[
                pl.BlockSpec((1, qr, KV_DIM), per_b),
                pl.BlockSpec((1, rows, KV_DIM), per_b),
                pl.BlockSpec((1, rows, STICK_KB), per_b),
            ] + _page_specs(layer, lambda s, j: n_pages - 1 - ((s + 1) * PAGES_PER_STEP + j)),
            out_specs=pl.BlockSpec((1, qr, KV_DIM), per_b),
            scratch_shapes=[pltpu.VMEM((rows, KV_DIM), _F32),
                            pltpu.VMEM((rows, STICK_KB), _F32),
                            pltpu.SMEM((1,), jnp.int32)],
        ),
        out_shape=jax.ShapeDtypeStruct(q_rows.shape, _BF16),
        compiler_params=params,
        name="stick_sample_rest",
    )
    return lax.cond(jnp.max(live) > 0,
                    lambda: rest(pt_rest, live, q_rows, acc, carry, *pages),
                    lambda: o_rows)


def _ffn_body(*refs, tm, tiles_per_seq, sample, final):
    it = iter(refs)
    x_ref, o_ref, wo_ref, g_ref, wup_ref, cw_ref, cb_ref, wdn_ref = (next(it) for _ in range(8))
    s1_ref = s2_ref = tok_ref = gfin_ref = None
    if sample:
        s1_ref, s2_ref, tok_ref = next(it), next(it), next(it)
    if final:
        gfin_ref = next(it)
    xo_ref, u_ref = next(it), next(it)
    y_ref = next(it) if final else None
    ubuf, carry, abuf = next(it), next(it), next(it)

    i = pl.program_id(0)
    x1 = x_ref[...] + jnp.dot(o_ref[...], wo_ref[...], preferred_element_type=_F32)
    ms = jnp.mean(x1 * x1, axis=-1, keepdims=True)
    h = ((x1 * lax.rsqrt(ms + RMS_EPS)) * g_ref[...]).astype(_BF16)
    xo_ref[...] = x1

    if not sample:
        @pl.when(i % tiles_per_seq == 0)
        def _():
            carry[...] = jnp.zeros(carry.shape, _F32)
    ubuf[:, 0:8, :] = jnp.zeros((ubuf.shape[0], 8, FF_CHUNK), _F32)

    def conv(jj, slot):
        u = jnp.dot(h, wup_ref[:, jj * FF_CHUNK:(jj + 1) * FF_CHUNK], preferred_element_type=_F32)
        buf = ubuf.at[slot]
        buf[8:8 + tm, :] = u
        if sample:
            u_ref[jj] = u
            tok = tok_ref[...]
            u1 = jnp.where(tok < 1.0, s1_ref[jj], buf[7:7 + tm, :])
            u2 = jnp.where(tok < 2.0, s2_ref[jj], buf[6:6 + tm, :])
        else:
            buf[6:8, :] = carry[jj, 6:8, :]
            u1 = buf[7:7 + tm, :]
            u2 = buf[6:6 + tm, :]
            carry[jj, 6:8, :] = u[tm - 2:tm, :]
        w = cw_ref[jj]
        return cb_ref[jj] + w[0:1, :] * u2 + w[1:2, :] * u1 + w[2:3, :] * u

    for j in range(N_FF_CHUNKS):
        cg = conv(j, 2 * (j % 2))
        cv = conv(j + N_FF_CHUNKS, 2 * (j % 2) + 1)
        act = (cg / (1.0 + jnp.exp(-cg))) * cv
        abuf[:, j * FF_CHUNK:(j + 1) * FF_CHUNK] = act.astype(_BF16)
    xo_ref[...] += jnp.dot(abuf[...], wdn_ref[...], preferred_element_type=_F32)
    if not sample:
        u_ref[...] = carry[...]
    if final:
        xo = xo_ref[...]
        ms2 = jnp.mean(xo * xo, axis=-1, keepdims=True)
        y_ref[...] = (xo * lax.rsqrt(ms2 + RMS_EPS)) * gfin_ref[...]


def _oproj_ffn(x, o, wo, g, wup, cw, cb, wdn, *, tm, seq, sample_state=None, g_final=None):
    t = x.shape[0]
    sample = sample_state is not None
    final = g_final is not None
    n_tiles = t // tm
    tiles_per_seq = max(seq // tm, 1)
    const2 = lambda i: (0, 0)
    const3 = lambda i: (0, 0, 0)
    once = pl.Buffered(1)
    in_specs = [
        pl.BlockSpec((tm, D_MODEL), lambda i: (i, 0)),
        pl.BlockSpec((tm, D_MODEL), lambda i: (i, 0)),
        pl.BlockSpec(wo.shape, const2, pipeline_mode=once),
        pl.BlockSpec((1, D_MODEL), const2),
        pl.BlockSpec(wup.shape, const2, pipeline_mode=once),
        pl.BlockSpec(cw.shape, const3),
        pl.BlockSpec(cb.shape, const3),
        pl.BlockSpec(wdn.shape, const2, pipeline_mode=once),
    ]
    args = [x, o, wo, g, wup, cw, cb, wdn]
    if sample:
        s1, s2, tok = sample_state
        in_specs += [pl.BlockSpec(s1.shape, const3), pl.BlockSpec(s2.shape, const3),
                     pl.BlockSpec(tok.shape, const2)]
        args += [s1, s2, tok]
    if final:
        in_specs.append(pl.BlockSpec((1, D_MODEL), const2))
        args.append(g_final)
    out_shape = [jax.ShapeDtypeStruct((t, D_MODEL), _F32)]
    out_specs = [pl.BlockSpec((tm, D_MODEL), lambda i: (i, 0))]
    if sample:
        out_shape.append(jax.ShapeDtypeStruct((2 * N_FF_CHUNKS, t, FF_CHUNK), _F32))
        out_specs.append(pl.BlockSpec((2 * N_FF_CHUNKS, tm, FF_CHUNK), lambda i: (0, i, 0)))
    else:
        n_seq = t // seq
        out_shape.append(jax.ShapeDtypeStruct((n_seq, 2 * N_FF_CHUNKS, 8, FF_CHUNK), _F32))
        out_specs.append(pl.BlockSpec((None, 2 * N_FF_CHUNKS, 8, FF_CHUNK),
                                      lambda i: (i // tiles_per_seq, 0, 0, 0)))
    if final:
        out_shape.append(jax.ShapeDtypeStruct((t, D_MODEL), _F32))
        out_specs.append(pl.BlockSpec((tm, D_MODEL), lambda i: (i, 0)))
    return pl.pallas_call(
        functools.partial(_ffn_body, tm=tm, tiles_per_seq=tiles_per_seq, sample=sample, final=final),
        grid=(n_tiles,),
        in_specs=in_specs,
        out_specs=out_specs,
        out_shape=out_shape,
        scratch_shapes=[pltpu.VMEM((4, tm + 8, FF_CHUNK), _F32),
                        pltpu.VMEM((2 * N_FF_CHUNKS, 8, FF_CHUNK), _F32),
                        pltpu.VMEM((tm, D_FF), _BF16)],
        compiler_params=pltpu.CompilerParams(dimension_semantics=("arbitrary",),
                                             vmem_limit_bytes=VMEM_LIMIT),
        name="oproj_ffn_sample" if sample else "oproj_ffn",
    )(*args)


def _chunk_cols(a):
    lead = a.shape[:-2]
    r = a.shape[-2]
    a = a.reshape(lead + (r, 2 * N_FF_CHUNKS, FF_CHUNK))
    return jnp.moveaxis(a, -2, -3)


def _unchunk_cols(a):
    a = jnp.moveaxis(a, -3, -2)
    return a.reshape(a.shape[:-2] + (2 * D_FF,))


def _bf16_pieces(x):
    hi = x.astype(_BF16).astype(_F32)
    mid = (x - hi).astype(_BF16).astype(_F32)
    lo = (x - hi - mid).astype(_BF16).astype(_F32)
    return hi, mid, lo


def kernel(x_prompt, x_sample, cache_k, cache_v, state_conv, page_table, g_attn, w_qkv, w_o,
           g_ffn, w_up, conv_w, conv_b, w_down, g_final):
    depth = w_qkv.shape[0]
    batch, seq, _ = x_prompt.shape
    dec_b, dec_s, _ = x_sample.shape
    n_pool = cache_k.shape[1]
    n_pages = page_table.shape[1]
    past = n_pages * PAGE_SIZE
    tp = batch * seq
    ts = dec_b * dec_s
    nb = seq // MOBA_BLOCK

    wqkv = w_qkv.astype(_BF16)
    wo = w_o.astype(_BF16)
    wup = w_up.astype(_BF16)
    cw = _chunk_cols(conv_w)
    cb = _chunk_cols(conv_b[:, None, :])
    wdn = w_down.astype(_BF16)
    g_attn2 = g_attn[:, None, :]
    g_ffn2 = g_ffn[:, None, :]
    g_fin2 = g_final[None, :]

    hidx = jnp.arange(1, N_HEADS + 1, dtype=_F32)
    slopes = jnp.exp2(-8.0 * hidx / N_HEADS)
    assert nb <= SLOPE_LANE0 - FLAG_LANE0 and past % MOBA_BLOCK == 0 and dec_s <= PAGE_SIZE
    pieces = jnp.stack(_bf16_pieces(slopes * LOG2E) * 2, axis=1)
    stab = jnp.zeros((N_HEADS, 8, SLAB), _F32).at[:, :, SLOPE_LANE0:SLOPE_LANE0 + 6].set(pieces[:, None, :])
    slope_rows = jnp.repeat(slopes, dec_s)[:, None]
    tok_rows = jnp.tile(jnp.arange(dec_s, dtype=_F32), N_HEADS)[:, None]
    qpos_rows = tok_rows + float(past)
    tok_seq = jnp.tile(jnp.arange(dec_s, dtype=_F32), dec_b)[:, None]

    cache_kt = cache_k.transpose(0, 1, 3, 4, 2).reshape(depth, n_pool, KV_DIM, PAGE_SIZE)
    cache_vt = cache_v.transpose(0, 1, 3, 4, 2).reshape(depth, n_pool, KV_DIM, PAGE_SIZE)

    st = state_conv
    zero = jnp.zeros_like(st[:, :, :1])
    s1 = jnp.concatenate([st[:, :, 1:2], zero, zero, zero][:dec_s], axis=2)
    s2 = jnp.concatenate([st[:, :, 0:1], st[:, :, 1:2], zero, zero][:dec_s], axis=2)
    s1 = _chunk_cols(s1.reshape(depth, ts, 2 * D_FF))
    s2 = _chunk_cols(s2.reshape(depth, ts, 2 * D_FF))

    xp = x_prompt.reshape(tp, D_MODEL)
    xs = x_sample.reshape(ts, D_MODEL)
    ks_l, vs_l, cp_l, cs_l = [], [], [], []
    k_all = jnp.zeros((depth, batch, KV_DIM, seq), _F32)
    v_all = jnp.zeros((depth, batch, KV_DIM, seq), _F32)
    yp = ys = None
    pad_own = jnp.zeros((dec_b, PAGE_SIZE - dec_s, KV_DIM), _BF16)
    for l in range(depth):
        last = l == depth - 1
        gfin = g_fin2 if last else None
        moba = l % 2 == 0
        q, k_all, v_all, kslab, vslab, *km = _rms_qkv(xp, g_attn2[l], wqkv[l], tm=512, with_kmean=moba,
                                                      kv_stack=(k_all, v_all), layer=l)
        if moba:
            kmt = km[0].reshape(batch, nb, N_KV_HEADS, HEAD_DIM).transpose(0, 2, 1, 3)
            kmt = jnp.pad(kmt, ((0, 0), (0, 0), (0, SLOPE_LANE0 - FLAG_LANE0 - nb),
                                (0, SLAB - HEAD_DIM)))
            knorm = jnp.sqrt(jnp.max(km[1].reshape(batch, nb, N_KV_HEADS, HEAD_DIM), axis=(1, 3)))
            o = _moba_prompt(knorm.reshape(batch * N_KV_HEADS), stab, q, kslab, vslab, kmt,
                             batch=batch, seq=seq)
        else:
            o = _stick_prompt(q, kslab, vslab, batch=batch, seq=seq)
        outs = _oproj_ffn(xp, o, wo[l], g_ffn2[l], wup[l], cw[l], cb[l], wdn[l],
                          tm=512, seq=seq, g_final=gfin)
        xp, cst = outs[0], outs[1]
        if last:
            yp = outs[2]
        cp_l.append(_unchunk_cols(cst[:, :, 6:8, :]))
        qs, ksn, vsn, _, _ = _rms_qkv(xs, g_attn2[l], wqkv[l], tm=ts, with_kmean=False)
        qnat = qs[:, :, :HEAD_DIM].reshape(N_KV_HEADS, GROUP, dec_b, dec_s, HEAD_DIM)
        q_rows = qnat.transpose(2, 1, 3, 0, 4).reshape(dec_b, GROUP * dec_s, KV_DIM)
        k_own = jnp.concatenate([ksn.astype(_BF16).reshape(dec_b, dec_s, KV_DIM), pad_own], axis=1)
        v_own = jnp.concatenate([vsn.astype(_BF16).reshape(dec_b, dec_s, KV_DIM), pad_own], axis=1)
        if moba:
            o_rows = _moba_sample(page_table, slope_rows, qpos_rows, q_rows, k_own, v_own,
                                  cache_kt, cache_vt, layer=l)
        else:
            o_rows = _stick_sample(page_table, tok_rows, q_rows, k_own, v_own, cache_kt, cache_vt, layer=l)
        os_ = o_rows.reshape(dec_b, GROUP, dec_s, N_KV_HEADS, HEAD_DIM).transpose(0, 2, 3, 1, 4)
        outs = _oproj_ffn(xs, os_.reshape(ts, D_MODEL), wo[l], g_ffn2[l], wup[l], cw[l], cb[l], wdn[l],
                          tm=ts, seq=dec_s, sample_state=(s1[l], s2[l], tok_seq), g_final=gfin)
        xs, u_all = outs[0], outs[1]
        if last:
            ys = outs[2]
        ks_l.append(ksn.reshape(dec_b, dec_s, N_KV_HEADS, HEAD_DIM))
        vs_l.append(vsn.reshape(dec_b, dec_s, N_KV_HEADS, HEAD_DIM))
        u_full = _unchunk_cols(u_all).reshape(dec_b, dec_s, 2 * D_FF)
        cs_l.append(u_full[:, dec_s - (CONV_W - 1):, :])
    return (yp.reshape(batch, seq, D_MODEL), ys.reshape(dec_b, dec_s, D_MODEL),
            k_all.reshape(depth, batch, N_KV_HEADS, HEAD_DIM, seq).transpose(0, 1, 4, 2, 3),
            v_all.reshape(depth, batch, N_KV_HEADS, HEAD_DIM, seq).transpose(0, 1, 4, 2, 3),
            jnp.stack(ks_l), jnp.stack(vs_l),
            jnp.stack(cp_l), jnp.stack(cs_l))
```

```python
import functools

import jax
import jax.numpy as jnp
from jax import lax
from jax.experimental import pallas as pl
from jax.experimental.pallas import tpu as pltpu

D_MODEL = 1024
N_HEADS = 16
HEAD_DIM = 64
N_KV_HEADS = 4
GROUP = N_HEADS // N_KV_HEADS
KV_DIM = N_KV_HEADS * HEAD_DIM
SLAB = 2 * HEAD_DIM
MOBA_BLOCK = 256
MOBA_TOPK = 3
D_FF = 2816
CONV_W = 3
PAGE_SIZE = 128
RMS_EPS = 1e-6
NEG_INF = -1e30
SCALE = HEAD_DIM ** -0.5

FF_CHUNK = 256
N_FF_CHUNKS = D_FF // FF_CHUNK
STICK_KB = 128
LOG2E = 1.4426950408889634
STICK_DEAD = -110.0 * LOG2E
MOBA_DEAD = 152.0
NORM_SLACK = 1.04
GATE_LANES = 128
FLAG_LANE0 = HEAD_DIM
SLOPE_LANE0 = 96
PAGES_PER_STEP = 8
MOBA_PAGES_PER_STEP = 16
VMEM_LIMIT = 56 * 1024 * 1024
TQ = 256
STICK_TQ = 256
QKV_TM = 1024
FFN_TM = 512

_BF16 = jnp.bfloat16
_F32 = jnp.float32


def _nt_dot(a, b):
    return lax.dot_general(a, b, (((1,), (1,)), ((), ())), preferred_element_type=_F32)


def _lane_group(shape):
    return lax.broadcasted_iota(jnp.int32, shape, len(shape) - 1) >> 6


def _qkv_body(*refs, tm, stacked):
    if stacked:
        x_ref, g_ref, w_ref, _, _, q_ref, k_ref, v_ref, ks_ref, vs_ref, *km_ref = refs
    else:
        x_ref, g_ref, w_ref, q_ref, k_ref, v_ref, ks_ref, vs_ref, *km_ref = refs
    x = x_ref[...]
    ms = jnp.mean(x * x, axis=-1, keepdims=True)
    h = (x * lax.rsqrt(ms + RMS_EPS)) * g_ref[...]
    qkv = jnp.dot(h.astype(_BF16), w_ref[...], preferred_element_type=_F32)
    for j in range(N_HEADS // 2):
        pair = qkv[:, j * SLAB:(j + 1) * SLAB]
        q_ref[2 * j] = pair
        q_ref[2 * j + 1] = pltpu.roll(pair, HEAD_DIM, 1)
    k = qkv[:, D_MODEL:D_MODEL + KV_DIM]
    v = qkv[:, D_MODEL + KV_DIM:D_MODEL + 2 * KV_DIM]
    k_ref[...] = k.T if stacked else k
    v_ref[...] = v.T if stacked else v
    for j in range(N_KV_HEADS // 2):
        for src, dst in ((k, ks_ref), (v, vs_ref)):
            pair = src[:, j * SLAB:(j + 1) * SLAB]
            dst[2 * j] = pair.astype(_BF16)
            dst[2 * j + 1] = pltpu.roll(pair, HEAD_DIM, 1).astype(_BF16)
    if km_ref:
        li = lax.broadcasted_iota(jnp.int32, (KV_DIM, KV_DIM), 0) >> 6
        lj = lax.broadcasted_iota(jnp.int32, (KV_DIM, KV_DIM), 1) >> 6
        k2 = jnp.dot((k * k).astype(_BF16), jnp.where(li == lj, 1.0, 0.0).astype(_BF16),
                     preferred_element_type=_F32)
        for j in range(tm // MOBA_BLOCK):
            blk = slice(j * MOBA_BLOCK, (j + 1) * MOBA_BLOCK)
            km_ref[0][j] = jnp.mean(k[blk], axis=0, keepdims=True)
            km_ref[1][j] = jnp.max(k2[blk], axis=0, keepdims=True)


def _rms_qkv(x, g, w, *, tm, with_kmean, kv_stack=None, layer=0):
    t = x.shape[0]
    stacked = kv_stack is not None
    if stacked:
        seq = kv_stack[0].shape[3]
        tps = seq // tm
        kv_shape = [jax.ShapeDtypeStruct(kv_stack[0].shape, _F32)] * 2
        row = pl.BlockSpec((None, None, KV_DIM, tm), lambda i: (layer, i // tps, 0, i % tps))
    else:
        kv_shape = [jax.ShapeDtypeStruct((t, KV_DIM), _F32)] * 2
        row = pl.BlockSpec((tm, KV_DIM), lambda i: (i, 0))
    out_shape = [jax.ShapeDtypeStruct((N_HEADS, t, SLAB), _F32)] + kv_shape + [
        jax.ShapeDtypeStruct((N_KV_HEADS, t, SLAB), _BF16),
        jax.ShapeDtypeStruct((N_KV_HEADS, t, SLAB), _BF16),
    ]
    slab = pl.BlockSpec((N_KV_HEADS, tm, SLAB), lambda i: (0, i, 0))
    out_specs = [pl.BlockSpec((N_HEADS, tm, SLAB), lambda i: (0, i, 0)), row, row, slab, slab]
    if with_kmean:
        for _ in range(2):
            out_shape.append(jax.ShapeDtypeStruct((t // MOBA_BLOCK, 1, KV_DIM), _F32))
            out_specs.append(pl.BlockSpec((tm // MOBA_BLOCK, 1, KV_DIM), lambda i: (i, 0, 0)))
    in_specs = [
        pl.BlockSpec((tm, D_MODEL), lambda i: (i, 0)),
        pl.BlockSpec((1, D_MODEL), lambda i: (0, 0)),
        pl.BlockSpec(w.shape, lambda i: (0, 0)),
    ]
    args = [x, g, w]
    aliases = {}
    if stacked:
        in_specs += [pl.BlockSpec(memory_space=pl.ANY)] * 2
        args += list(kv_stack)
        aliases = {3: 1, 4: 2}
    return pl.pallas_call(
        functools.partial(_qkv_body, tm=tm, stacked=stacked),
        grid=(t // tm,),
        in_specs=in_specs,
        out_specs=out_specs,
        out_shape=out_shape,
        input_output_aliases=aliases,
        compiler_params=pltpu.CompilerParams(dimension_semantics=("arbitrary",),
                                             vmem_limit_bytes=VMEM_LIMIT),
        name="rms_qkv",
    )(*args)


def _top3_select(gate, eligible, axis):
    idx = lax.broadcasted_iota(jnp.int32, gate.shape, axis).astype(_F32)
    neg = jnp.float32(-jnp.inf)
    gm = jnp.where(eligible, gate, neg)
    sel = jnp.zeros(gate.shape, _F32)
    for _ in range(MOBA_TOPK):
        mx = jnp.max(gm, axis=axis, keepdims=True)
        first = jnp.min(jnp.where(gm == mx, idx, jnp.float32(1e9)), axis=axis, keepdims=True)
        pick = jnp.logical_and(idx == first, mx > neg)
        sel = jnp.where(pick, 1.0, sel)
        gm = jnp.where(pick, neg, gm)
    return sel


def _stick_tri():
    j = lax.broadcasted_iota(jnp.int32, (2 * STICK_KB, 2 * STICK_KB), 0) & (STICK_KB - 1)
    s = lax.broadcasted_iota(jnp.int32, (2 * STICK_KB, 2 * STICK_KB), 1)
    return jnp.where(jnp.logical_or(s >= STICK_KB, j > s), 1.0, 0.0).astype(_BF16)


def _stick_chunk(z, causal, carry, tri, vc_nt=None, vc=None):
    nz = -z
    e = jnp.exp2(jnp.minimum(z, nz))
    log_keep = jnp.minimum(nz, 0.0) - jnp.log2(1.0 + e)
    log_sig = z + log_keep
    if causal is not None:
        log_keep = jnp.where(causal, log_keep, 0.0)
    hi = log_keep.astype(_BF16)
    lo = (log_keep - hi.astype(_F32)).astype(_BF16)
    ct = jnp.dot(jnp.concatenate([hi, lo], axis=1), tri, preferred_element_type=_F32)
    log_after = carry + ct[:, :STICK_KB]
    a = jnp.exp2(log_sig + log_after)
    if causal is not None:
        a = jnp.where(causal, a, 0.0)
    a = a.astype(_BF16)
    pv = _nt_dot(a, vc_nt) if vc is None else jnp.dot(a, vc, preferred_element_type=_F32)
    return pv, carry + ct[:, STICK_KB:]


def _pair_heads(acc_ref, o_ref, normalise):
    low = lax.broadcasted_iota(jnp.int32, acc_ref.shape[1:], 1) < HEAD_DIM
    for j in range(N_HEADS // 2):
        a0 = acc_ref[2 * j]
        a1 = acc_ref[2 * j + 1]
        out = jnp.where(low, a0, pltpu.roll(a1, HEAD_DIM, 1))
        if normalise:
            out = out / jnp.where(low, pltpu.roll(a0, HEAD_DIM, 1), a1)
        o_ref[:, j * SLAB:(j + 1) * SLAB] = out.astype(o_ref.dtype)


def _moba_prompt_body(kmax_ref, stab_ref, q_ref, k_ref, v_ref, km_ref, o_ref, qa_ref, m_ref, acc_ref, *, tq):
    i = pl.program_id(1)
    rows = GROUP * tq
    low = lax.broadcasted_iota(jnp.int32, (tq, SLAB), 1) < HEAD_DIM
    n_flags = SLOPE_LANE0 - FLAG_LANE0
    eligible = lax.broadcasted_iota(jnp.int32, (n_flags, tq), 0) < i
    krow = lax.broadcasted_iota(jnp.int32, (MOBA_BLOCK, SLAB), 0)
    klane = lax.broadcasted_iota(jnp.int32, (MOBA_BLOCK, SLAB), 1)
    klow = klane < HEAD_DIM
    one = jnp.ones((MOBA_BLOCK, SLAB), _BF16)

    def key_aux(block_offset, c):
        f = jnp.where(klane == FLAG_LANE0 + c, NEG_INF, 0.0)
        f = jnp.where(jnp.logical_and(klane >= SLOPE_LANE0, klane < SLOPE_LANE0 + 3), block_offset, f)
        f = jnp.where(jnp.logical_and(klane >= SLOPE_LANE0 + 3, klane < SLOPE_LANE0 + 6),
                      krow.astype(_F32), f)
        return f.astype(_BF16)

    def kv_head(n, _):
        km = km_ref[n]
        for g in range(GROUP):
            h = n * GROUP + g
            qh = q_ref[h]
            gate_t = lax.dot_general(km, qh, (((1,), (1,)), ((), ())),
                                     precision=lax.Precision.HIGHEST, preferred_element_type=_F32)
            sel_t = _top3_select(gate_t, eligible, 0)
            notsel_t = jnp.where(eligible, 1.0 - sel_t, 0.0)
            notsel = jnp.concatenate([jnp.zeros((FLAG_LANE0, tq), _F32), notsel_t,
                                      jnp.zeros((GATE_LANES - FLAG_LANE0 - n_flags, tq), _F32)],
                                     axis=0).T
            q2 = qh * (SCALE * LOG2E)
            qa_ref[h] = jnp.where(low, q2, notsel + stab_ref[h][0:1, :]).astype(_BF16)
            qn = jnp.sqrt(jnp.sum(jnp.where(low, q2 * q2, 0.0), axis=1, keepdims=True))
            m_ref[h] = jnp.broadcast_to(qn, (tq, SLAB))
        return 0

    lax.fori_loop(0, N_KV_HEADS, kv_head, 0)

    off = pl.multiple_of(i * MOBA_BLOCK, MOBA_BLOCK)
    f_own = key_aux(0.0, i)
    qrow = lax.broadcasted_iota(jnp.int32, (rows, MOBA_BLOCK), 0) & (tq - 1)
    kcol = lax.broadcasted_iota(jnp.int32, (rows, MOBA_BLOCK), 1)
    causal = qrow >= kcol
    keep = []
    for n in range(N_KV_HEADS):
        hs = pl.ds(n * GROUP, GROUP)
        kk = jnp.where(klow, k_ref[n, pl.ds(off, MOBA_BLOCK), :], f_own)
        vv = jnp.where(klow, v_ref[n, pl.ds(off, MOBA_BLOCK), :], one)
        s = _nt_dot(qa_ref[hs].reshape(rows, SLAB), kk)
        s = jnp.where(causal, s, NEG_INF)
        m0 = jnp.max(s, axis=1, keepdims=True)
        p = jnp.exp2(s - m0)
        qn = m_ref[hs].reshape(rows, SLAB)[:, 0:1]
        m_ref[hs] = jnp.broadcast_to(m0, (rows, SLAB)).reshape(GROUP, tq, SLAB)
        acc_ref[hs] = jnp.dot(p.astype(_BF16), vv, preferred_element_type=_F32).reshape(GROUP, tq, SLAB)
        slope_min = 2.0 ** (-8.0 * (n * GROUP + GROUP) / N_HEADS) * LOG2E * 0.999
        reach = (qn * (kmax_ref[pl.program_id(0) * N_KV_HEADS + n] * NORM_SLACK) + MOBA_DEAD - m0)
        blocks = jnp.max(reach) * (1.0 / (MOBA_BLOCK * slope_min)) - 1.0 / MOBA_BLOCK
        k_n = jnp.int32(0)
        for d in range(n_flags):
            k_n = k_n + (blocks >= float(d)).astype(jnp.int32)
        keep.append(k_n if n == 0 else jnp.maximum(k_n, keep[-1]))

    def past(c, _, groups):
        offc = pl.multiple_of(c * MOBA_BLOCK, MOBA_BLOCK)
        f_c = key_aux(((c - i) * MOBA_BLOCK).astype(_F32), c)
        for n in groups:
            hs = pl.ds(n * GROUP, GROUP)
            kk = jnp.where(klow, k_ref[n, pl.ds(offc, MOBA_BLOCK), :], f_c)
            vv = jnp.where(klow, v_ref[n, pl.ds(offc, MOBA_BLOCK), :], one)
            s = _nt_dot(qa_ref[hs].reshape(rows, SLAB), kk)
            m_old = m_ref[hs].reshape(rows, SLAB)
            m_new = jnp.maximum(m_old, jnp.max(s, axis=1, keepdims=True))
            alpha = jnp.exp2(m_old - m_new)
            p = jnp.exp2(s - jnp.concatenate([m_new, m_new], axis=1))
            pv = jnp.dot(p.astype(_BF16), vv, preferred_element_type=_F32)
            m_ref[hs] = m_new.reshape(GROUP, tq, SLAB)
            acc_ref[hs] = (alpha * acc_ref[hs].reshape(rows, SLAB) + pv).reshape(GROUP, tq, SLAB)
        return 0

    start = [jnp.maximum(i - k_n, 0) for k_n in keep]
    for n in range(N_KV_HEADS - 1, -1, -1):
        stop = start[n - 1] if n > 0 else i
        body = functools.partial(past, groups=tuple(range(n, N_KV_HEADS)))
        if n == 0:
            lax.fori_loop(start[n], stop, body, 0)
            continue
        pairs = (stop - start[n]) // 2

        def two(t, _, body=body, first=start[n]):
            body(first + 2 * t, 0)
            body(first + 2 * t + 1, 0)
            return 0

        lax.fori_loop(0, pairs, two, 0)
        lax.fori_loop(start[n] + 2 * pairs, stop, body, 0)
    _pair_heads(acc_ref, o_ref, normalise=True)


def _moba_prompt(kmax, stab, q, ks, vs, kmtab, *, batch, seq):
    tq = TQ
    nq = seq // tq
    return pl.pallas_call(
        functools.partial(_moba_prompt_body, tq=tq),
        grid=(batch, nq),
        in_specs=[
            pl.BlockSpec(memory_space=pltpu.SMEM),
            pl.BlockSpec(stab.shape, lambda b, i: (0, 0, 0)),
            pl.BlockSpec((N_HEADS, tq, SLAB), lambda b, i: (0, b * nq + i, 0)),
            pl.BlockSpec((N_KV_HEADS, seq, SLAB), lambda b, i: (0, b, 0)),
            pl.BlockSpec((N_KV_HEADS, seq, SLAB), lambda b, i: (0, b, 0)),
            pl.BlockSpec((None,) + kmtab.shape[1:], lambda b, i: (b, 0, 0, 0)),
        ],
        out_specs=pl.BlockSpec((tq, D_MODEL), lambda b, i: (b * nq + i, 0)),
        out_shape=jax.ShapeDtypeStruct((batch * seq, D_MODEL), _BF16),
        scratch_shapes=[pltpu.VMEM((N_HEADS, tq, SLAB), _BF16),
                        pltpu.VMEM((N_HEADS, tq, SLAB), _F32),
                        pltpu.VMEM((N_HEADS, tq, SLAB), _F32)],
        compiler_params=pltpu.CompilerParams(dimension_semantics=("arbitrary", "arbitrary"),
                                             vmem_limit_bytes=VMEM_LIMIT),
        name="moba_prompt",
    )(kmax, stab, q, ks, vs, kmtab)


def _stick_prompt_body(q_ref, k_ref, v_ref, o_ref, qa_ref, acc_ref, carry_ref, *, tq):
    i = pl.program_id(1)
    rows = GROUP * tq
    low = lax.broadcasted_iota(jnp.int32, q_ref.shape, 2) < HEAD_DIM
    qa_ref[...] = jnp.where(low, q_ref[...] * (SCALE * LOG2E), 0.0).astype(_BF16)
    acc_ref[...] = jnp.zeros(acc_ref.shape, _F32)
    carry_ref[...] = jnp.zeros(carry_ref.shape, _F32)
    tri = _stick_tri()
    qrow = lax.broadcasted_iota(jnp.int32, (rows, STICK_KB), 0) & (tq - 1)
    kcol = lax.broadcasted_iota(jnp.int32, (rows, STICK_KB), 1)
    rel = qrow - kcol
    chunks_per_tile = tq // STICK_KB

    def chunk(c, masked):
        off = pl.multiple_of(c * STICK_KB, STICK_KB)
        causal = (rel + (i * tq - c * STICK_KB) > 0) if masked else None
        top = jnp.float32(-jnp.inf)
        for n in range(N_KV_HEADS):
            hs = pl.ds(n * GROUP, GROUP)
            z = _nt_dot(qa_ref[hs].reshape(rows, SLAB), k_ref[n, pl.ds(off, STICK_KB), :])
            pv, carry = _stick_chunk(z, causal, carry_ref[hs].reshape(rows, STICK_KB), tri,
                                     vc=v_ref[n, pl.ds(off, STICK_KB), :])
            acc_ref[hs] += pv.reshape(GROUP, tq, SLAB)
            carry_ref[hs] = carry.reshape(GROUP, tq, STICK_KB)
            top = jnp.maximum(top, jnp.max(carry))
        return top

    top = jnp.float32(0.0)
    for d in range(chunks_per_tile):
        top = chunk((i + 1) * chunks_per_tile - 1 - d, True)

    def cond(state):
        c, alive = state
        return jnp.logical_and(c >= 0, alive > 0)

    def step(state):
        c, _ = state
        return c - 1, (chunk(c, False) > STICK_DEAD).astype(jnp.int32)

    lax.while_loop(cond, step, (i * chunks_per_tile - 1, (top > STICK_DEAD).astype(jnp.int32)))
    _pair_heads(acc_ref, o_ref, normalise=False)


def _stick_prompt(q, ks, vs, *, batch, seq):
    tq = STICK_TQ
    nq = seq // tq
    return pl.pallas_call(
        functools.partial(_stick_prompt_body, tq=tq),
        grid=(batch, nq),
        in_specs=[
            pl.BlockSpec((N_HEADS, tq, SLAB), lambda b, i: (0, b * nq + i, 0)),
            pl.BlockSpec((N_KV_HEADS, seq, SLAB), lambda b, i: (0, b, 0)),
            pl.BlockSpec((N_KV_HEADS, seq, SLAB), lambda b, i: (0, b, 0)),
        ],
        out_specs=pl.BlockSpec((tq, D_MODEL), lambda b, i: (b * nq + i, 0)),
        out_shape=jax.ShapeDtypeStruct((batch * seq, D_MODEL), _BF16),
        scratch_shapes=[pltpu.VMEM((N_HEADS, tq, SLAB), _BF16),
                        pltpu.VMEM((N_HEADS, tq, SLAB), _F32),
                        pltpu.VMEM((N_HEADS, tq, STICK_KB), _F32)],
        compiler_params=pltpu.CompilerParams(dimension_semantics=("arbitrary", "arbitrary"),
                                             vmem_limit_bytes=VMEM_LIMIT),
        name="stick_prompt",
    )(q, ks, vs)


def _sample_rows(q_ref, scale):
    q = q_ref[0] * scale
    grp = _lane_group(q.shape)
    return jnp.concatenate([jnp.where(grp == n, q, 0.0) for n in range(N_KV_HEADS)], axis=0)


def _sample_out(acc):
    r = acc.shape[0] // N_KV_HEADS
    grp = _lane_group((r, KV_DIM))
    out = jnp.zeros((r, KV_DIM), _F32)
    for n in range(N_KV_HEADS):
        out = jnp.where(grp == n, acc[n * r:(n + 1) * r], out)
    return out


def _moba_sample_body(pt_ref, slope_ref, qpos_ref, q_ref, kown_ref, vown_ref, *rest, n_steps, pps):
    kp = rest[:pps]
    vp = rest[pps:2 * pps]
    o_ref, oall_ref, mtab_ref, ltab_ref, kmt_ref = rest[2 * pps:]
    sp = pl.program_id(1)
    qm = _sample_rows(q_ref, SCALE)
    qb = qm.astype(_BF16)
    rows = qm.shape[0]
    slope = slope_ref[...]
    qpos = qpos_ref[...]
    gcol = lax.broadcasted_iota(jnp.int32, (rows, GATE_LANES), 1)
    kmcol = lax.broadcasted_iota(jnp.int32, (KV_DIM, GATE_LANES), 1)
    kcol = lax.broadcasted_iota(jnp.int32, (rows, MOBA_BLOCK), 1).astype(_F32)
    pages_per_block = MOBA_BLOCK // PAGE_SIZE
    blocks_per_step = pps // pages_per_block

    @pl.when(sp == 0)
    def _():
        for tab in (mtab_ref, ltab_ref, kmt_ref):
            tab[...] = jnp.zeros(tab.shape, _F32)

    kmt, mtab, ltab = kmt_ref[...], mtab_ref[...], ltab_ref[...]
    for jb in range(blocks_per_step):
        m = sp * blocks_per_step + jb
        kblk = jnp.concatenate([kp[jb * pages_per_block + u][...] for u in range(pages_per_block)], axis=1)
        vblk = jnp.concatenate([vp[jb * pages_per_block + u][...] for u in range(pages_per_block)], axis=1)
        ksum = jnp.sum(kblk, axis=1, keepdims=True)
        kmt = jnp.where(kmcol == m, ksum * (1.0 / MOBA_BLOCK), kmt)
        kpos = kcol + (m * MOBA_BLOCK).astype(_F32)
        s = jnp.dot(qb, kblk.astype(_BF16), preferred_element_type=_F32) - slope * (qpos - kpos)
        mb = jnp.max(s, axis=1, keepdims=True)
        p = jnp.exp(s - mb)
        oall_ref[m] = _nt_dot(p.astype(_BF16), vblk.astype(_BF16))
        here = gcol == m
        mtab = jnp.where(here, mb, mtab)
        ltab = jnp.where(here, jnp.sum(p, axis=1, keepdims=True), ltab)
    kmt_ref[...] = kmt
    mtab_ref[...] = mtab
    ltab_ref[...] = ltab

    @pl.when(sp == n_steps - 1)
    def _():
        n_blocks = n_steps * blocks_per_step
        ocol = lax.broadcasted_iota(jnp.int32, (rows, PAGE_SIZE), 1).astype(_F32)
        dist = (qpos - (n_blocks * MOBA_BLOCK)) - ocol
        s = _nt_dot(qb, kown_ref[0]) - slope * dist
        s = jnp.where(dist >= 0.0, s, NEG_INF)
        m_own = jnp.max(s, axis=1, keepdims=True)
        p = jnp.exp(s - m_own)
        l_own = jnp.sum(p, axis=1, keepdims=True)
        o_own = jnp.dot(p.astype(_BF16), vown_ref[0], preferred_element_type=_F32)

        gate = jnp.dot(qm, kmt_ref[...], precision=lax.Precision.HIGHEST, preferred_element_type=_F32)
        picked = _top3_select(gate, gcol < n_blocks, 1) > 0.0
        mtab = mtab_ref[...]
        m_all = jnp.maximum(m_own, jnp.max(jnp.where(picked, mtab, NEG_INF), axis=1, keepdims=True))
        w = jnp.where(picked, jnp.exp(mtab - m_all), 0.0)
        w_own = jnp.exp(m_own - m_all)
        l_all = w_own * l_own + jnp.sum(w * ltab_ref[...], axis=1, keepdims=True)

        def merge(mi, acc):
            wm = jnp.sum(jnp.where(gcol == mi, w, 0.0), axis=1, keepdims=True)
            return acc + wm * oall_ref[mi]

        acc = lax.fori_loop(0, n_blocks, merge, w_own * o_own)
        o_ref[0] = _sample_out(acc / l_all).astype(o_ref.dtype)


def _page_specs(layer, page_of, pps=PAGES_PER_STEP):
    def spec(j):
        return pl.BlockSpec((None, None, KV_DIM, PAGE_SIZE),
                            lambda b, s, pt, *_: (layer, pt[b, page_of(s, j)], 0, 0))
    return [spec(j) for j in range(pps)] * 2


def _moba_sample(page_table, slope_rows, qpos_rows, q_rows, k_own, v_own, cache_kt, cache_vt, *, layer):
    dec_b, n_pages = page_table.shape
    pps = MOBA_PAGES_PER_STEP if n_pages % MOBA_PAGES_PER_STEP == 0 else PAGES_PER_STEP
    n_steps = n_pages // pps
    rows = q_rows.shape[1] * N_KV_HEADS
    n_blocks = n_pages * PAGE_SIZE // MOBA_BLOCK
    grid_spec = pltpu.PrefetchScalarGridSpec(
        num_scalar_prefetch=1,
        grid=(dec_b, n_steps),
        in_specs=[
            pl.BlockSpec((rows, 1), lambda b, s, pt: (0, 0)),
            pl.BlockSpec((rows, 1), lambda b, s, pt: (0, 0)),
            pl.BlockSpec((1, q_rows.shape[1], KV_DIM), lambda b, s, pt: (b, 0, 0)),
            pl.BlockSpec((1, PAGE_SIZE, KV_DIM), lambda b, s, pt: (b, 0, 0)),
            pl.BlockSpec((1, PAGE_SIZE, KV_DIM), lambda b, s, pt: (b, 0, 0)),
        ] + _page_specs(layer, lambda s, j: s * pps + j, pps),
        out_specs=pl.BlockSpec((1, q_rows.shape[1], KV_DIM), lambda b, s, pt: (b, 0, 0)),
        scratch_shapes=[pltpu.VMEM((n_blocks, rows, KV_DIM), _F32),
                        pltpu.VMEM((rows, GATE_LANES), _F32),
                        pltpu.VMEM((rows, GATE_LANES), _F32),
                        pltpu.VMEM((KV_DIM, GATE_LANES), _F32)],
    )
    return pl.pallas_call(
        functools.partial(_moba_sample_body, n_steps=n_steps, pps=pps),
        grid_spec=grid_spec,
        out_shape=jax.ShapeDtypeStruct(q_rows.shape, _BF16),
        compiler_params=pltpu.CompilerParams(dimension_semantics=("arbitrary", "arbitrary"),
                                             vmem_limit_bytes=VMEM_LIMIT),
        name="moba_sample",
    )(page_table, slope_rows, qpos_rows, q_rows, k_own, v_own,
      *([cache_kt] * pps), *([cache_vt] * pps))


def _stick_pages(qb, kp, vp, tri, acc_ref, carry_ref, alive_ref):
    for j in range(PAGES_PER_STEP):
        @pl.when(alive_ref[0] > 0)
        def _():
            z = jnp.dot(qb, kp[j][...].astype(_BF16), preferred_element_type=_F32)
            pv, carry = _stick_chunk(z, None, carry_ref[...], tri, vc_nt=vp[j][...].astype(_BF16))
            acc_ref[...] += pv
            carry_ref[...] = carry
            alive_ref[0] = (jnp.max(carry) > STICK_DEAD).astype(jnp.int32)


def _stick_sample_first_body(pt_ref, tok_ref, q_ref, kown_ref, vown_ref, *rest):
    kp = rest[:PAGES_PER_STEP]
    vp = rest[PAGES_PER_STEP:2 * PAGES_PER_STEP]
    o_ref, acc_ref, carry_ref, alive_ref = rest[2 * PAGES_PER_STEP:]
    qb = _sample_rows(q_ref, SCALE * LOG2E).astype(_BF16)
    rows = qb.shape[0]
    tri = _stick_tri()
    kcol = lax.broadcasted_iota(jnp.int32, (rows, STICK_KB), 1).astype(_F32)
    causal = kcol < tok_ref[...]
    z = _nt_dot(qb, kown_ref[0])
    pv, carry = _stick_chunk(z, causal, jnp.zeros((rows, STICK_KB), _F32), tri, vc=vown_ref[0])
    acc_ref[0] = pv
    carry_ref[0] = carry
    alive_ref[0] = (jnp.max(carry) > STICK_DEAD).astype(jnp.int32)
    _stick_pages(qb, kp, vp, tri, acc_ref.at[0], carry_ref.at[0], alive_ref)
    o_ref[0] = _sample_out(acc_ref[0]).astype(o_ref.dtype)


def _stick_sample_rest_body(pt_ref, live_ref, q_ref, acc_in_ref, carry_in_ref, *rest, n_steps):
    kp = rest[:PAGES_PER_STEP]
    vp = rest[PAGES_PER_STEP:2 * PAGES_PER_STEP]
    o_ref, acc_ref, carry_ref, alive_ref = rest[2 * PAGES_PER_STEP:]
    sp = pl.program_id(1)
    qb = _sample_rows(q_ref, SCALE * LOG2E).astype(_BF16)

    @pl.when(sp == 0)
    def _():
        acc_ref[...] = acc_in_ref[0]
        carry_ref[...] = carry_in_ref[0]
        alive_ref[0] = live_ref[pl.program_id(0)]

    _stick_pages(qb, kp, vp, _stick_tri(), acc_ref, carry_ref, alive_ref)

    @pl.when(sp == n_steps - 1)
    def _():
        o_ref[0] = _sample_out(acc_ref[...]).astype(o_ref.dtype)


def _stick_sample(page_table, tok_rows, q_rows, k_own, v_own, cache_kt, cache_vt, *, layer):
    dec_b, n_pages = page_table.shape
    n_steps = n_pages // PAGES_PER_STEP
    qr = q_rows.shape[1]
    rows = qr * N_KV_HEADS
    per_b = lambda b, s, *_: (b, 0, 0)
    params = pltpu.CompilerParams(dimension_semantics=("arbitrary", "arbitrary"),
                                  vmem_limit_bytes=VMEM_LIMIT)
    pages = [cache_kt] * PAGES_PER_STEP + [cache_vt] * PAGES_PER_STEP
    o_rows, acc, carry = pl.pallas_call(
        _stick_sample_first_body,
        grid_spec=pltpu.PrefetchScalarGridSpec(
            num_scalar_prefetch=1,
            grid=(dec_b, 1),
            in_specs=[
                pl.BlockSpec((rows, 1), lambda b, s, *_: (0, 0)),
                pl.BlockSpec((1, qr, KV_DIM), per_b),
                pl.BlockSpec((1, PAGE_SIZE, KV_DIM), per_b),
                pl.BlockSpec((1, PAGE_SIZE, KV_DIM), per_b),
            ] + _page_specs(layer, lambda s, j: n_pages - 1 - j),
            out_specs=[pl.BlockSpec((1, qr, KV_DIM), per_b),
                       pl.BlockSpec((1, rows, KV_DIM), per_b),
                       pl.BlockSpec((1, rows, STICK_KB), per_b)],
            scratch_shapes=[pltpu.SMEM((1,), jnp.int32)],
        ),
        out_shape=[jax.ShapeDtypeStruct(q_rows.shape, _BF16),
                   jax.ShapeDtypeStruct((dec_b, rows, KV_DIM), _F32),
                   jax.ShapeDtypeStruct((dec_b, rows, STICK_KB), _F32)],
        compiler_params=params,
        name="stick_sample_first",
    )(page_table, tok_rows, q_rows, k_own, v_own, *pages)
    if n_steps == 1:
        return o_rows
    live = (jnp.max(carry, axis=(1, 2)) > STICK_DEAD).astype(jnp.int32)
    pt_rest = jnp.where(live[:, None] > 0, page_table, page_table[0, 0])
    rest = pl.pallas_call(
        functools.partial(_stick_sample_rest_body, n_steps=n_steps - 1),
        grid_spec=pltpu.PrefetchScalarGridSpec(
            num_scalar_prefetch=2,
            grid=(dec_b, n_steps - 1),
            in_specs=[
                pl.BlockSpec((1, qr, KV_DIM), per_b),
                pl.BlockSpec((1, rows, KV_DIM), per_b),
                pl.BlockSpec((1, rows, STICK_KB), per_b),
            ] + _page_specs(layer, lambda s, j: n_pages - 1 - ((s + 1) * PAGES_PER_STEP + j)),
            out_specs=pl.BlockSpec((1, qr, KV_DIM), per_b),
            scratch_shapes=[pltpu.VMEM((rows, KV_DIM), _F32),
                            pltpu.VMEM((rows, STICK_KB), _F32),
                            pltpu.SMEM((1,), jnp.int32)],
        ),
        out_shape=jax.ShapeDtypeStruct(q_rows.shape, _BF16),
        compiler_params=params,
        name="stick_sample_rest",
    )
    return lax.cond(jnp.max(live) > 0,
                    lambda: rest(pt_rest, live, q_rows, acc, carry, *pages),
                    lambda: o_rows)


def _ffn_body(*refs, tm, tiles_per_seq, sample, final):
    it = iter(refs)
    x_ref, o_ref, wo_ref, g_ref, wup_ref, cw_ref, cb_ref, wdn_ref = (next(it) for _ in range(8))
    s1_ref = s2_ref = tok_ref = gfin_ref = None
    if sample:
        s1_ref, s2_ref, tok_ref = next(it), next(it), next(it)
    if final:
        gfin_ref = next(it)
    xo_ref, u_ref = next(it), next(it)
    y_ref = next(it) if final else None
    ubuf, carry, abuf = next(it), next(it), next(it)

    i = pl.program_id(0)
    x1 = x_ref[...] + jnp.dot(o_ref[...], wo_ref[...], preferred_element_type=_F32)
    ms = jnp.mean(x1 * x1, axis=-1, keepdims=True)
    h = ((x1 * lax.rsqrt(ms + RMS_EPS)) * g_ref[...]).astype(_BF16)
    xo_ref[...] = x1

    if not sample:
        @pl.when(i % tiles_per_seq == 0)
        def _():
            carry[...] = jnp.zeros(carry.shape, _F32)
    ubuf[:, 0:8, :] = jnp.zeros((ubuf.shape[0], 8, FF_CHUNK), _F32)

    def conv(jj, slot):
        u = jnp.dot(h, wup_ref[:, jj * FF_CHUNK:(jj + 1) * FF_CHUNK], preferred_element_type=_F32)
        buf = ubuf.at[slot]
        buf[8:8 + tm, :] = u
        if sample:
            u_ref[jj] = u
            tok = tok_ref[...]
            u1 = jnp.where(tok < 1.0, s1_ref[jj], buf[7:7 + tm, :])
            u2 = jnp.where(tok < 2.0, s2_ref[jj], buf[6:6 + tm, :])
        else:
            buf[6:8, :] = carry[jj, 6:8, :]
            u1 = buf[7:7 + tm, :]
            u2 = buf[6:6 + tm, :]
            carry[jj, 6:8, :] = u[tm - 2:tm, :]
        w = cw_ref[jj]
        return cb_ref[jj] + w[0:1, :] * u2 + w[1:2, :] * u1 + w[2:3, :] * u

    for j in range(N_FF_CHUNKS):
        cg = conv(j, 2 * (j % 2))
        cv = conv(j + N_FF_CHUNKS, 2 * (j % 2) + 1)
        act = (cg / (1.0 + jnp.exp(-cg))) * cv
        abuf[:, j * FF_CHUNK:(j + 1) * FF_CHUNK] = act.astype(_BF16)
    xo_ref[...] += jnp.dot(abuf[...], wdn_ref[...], preferred_element_type=_F32)
    if not sample:
        u_ref[...] = carry[...]
    if final:
        xo = xo_ref[...]
        ms2 = jnp.mean(xo * xo, axis=-1, keepdims=True)
        y_ref[...] = (xo * lax.rsqrt(ms2 + RMS_EPS)) * gfin_ref[...]


def _oproj_ffn(x, o, wo, g, wup, cw, cb, wdn, *, tm, seq, sample_state=None, g_final=None):
    t = x.shape[0]
    sample = sample_state is not None
    final = g_final is not None
    n_tiles = t // tm
    tiles_per_seq = max(seq // tm, 1)
    const2 = lambda i: (0, 0)
    const3 = lambda i: (0, 0, 0)
    once = pl.Buffered(1)
    in_specs = [
        pl.BlockSpec((tm, D_MODEL), lambda i: (i, 0)),
        pl.BlockSpec((tm, D_MODEL), lambda i: (i, 0)),
        pl.BlockSpec(wo.shape, const2, pipeline_mode=once),
        pl.BlockSpec((1, D_MODEL), const2),
        pl.BlockSpec(wup.shape, const2, pipeline_mode=once),
        pl.BlockSpec(cw.shape, const3),
        pl.BlockSpec(cb.shape, const3),
        pl.BlockSpec(wdn.shape, const2, pipeline_mode=once),
    ]
    args = [x, o, wo, g, wup, cw, cb, wdn]
    if sample:
        s1, s2, tok = sample_state
        in_specs += [pl.BlockSpec(s1.shape, const3), pl.BlockSpec(s2.shape, const3),
                     pl.BlockSpec(tok.shape, const2)]
        args += [s1, s2, tok]
    if final:
        in_specs.append(pl.BlockSpec((1, D_MODEL), const2))
        args.append(g_final)
    out_shape = [jax.ShapeDtypeStruct((t, D_MODEL), _F32)]
    out_specs = [pl.BlockSpec((tm, D_MODEL), lambda i: (i, 0))]
    if sample:
        out_shape.append(jax.ShapeDtypeStruct((2 * N_FF_CHUNKS, t, FF_CHUNK), _F32))
        out_specs.append(pl.BlockSpec((2 * N_FF_CHUNKS, tm, FF_CHUNK), lambda i: (0, i, 0)))
    else:
        n_seq = t // seq
        out_shape.append(jax.ShapeDtypeStruct((n_seq, 2 * N_FF_CHUNKS, 8, FF_CHUNK), _F32))
        out_specs.append(pl.BlockSpec((None, 2 * N_FF_CHUNKS, 8, FF_CHUNK),
                                      lambda i: (i // tiles_per_seq, 0, 0, 0)))
    if final:
        out_shape.append(jax.ShapeDtypeStruct((t, D_MODEL), _F32))
        out_specs.append(pl.BlockSpec((tm, D_MODEL), lambda i: (i, 0)))
    return pl.pallas_call(
        functools.partial(_ffn_body, tm=tm, tiles_per_seq=tiles_per_seq, sample=sample, final=final),
        grid=(n_tiles,),
        in_specs=in_specs,
        out_specs=out_specs,
        out_shape=out_shape,
        scratch_shapes=[pltpu.VMEM((4, tm + 8, FF_CHUNK), _F32),
                        pltpu.VMEM((2 * N_FF_CHUNKS, 8, FF_CHUNK), _F32),
                        pltpu.VMEM((tm, D_FF), _BF16)],
        compiler_params=pltpu.CompilerParams(dimension_semantics=("arbitrary",),
                                             vmem_limit_bytes=VMEM_LIMIT),
        name="oproj_ffn_sample" if sample else "oproj_ffn",
    )(*args)


def _chunk_cols(a):
    lead = a.shape[:-2]
    r = a.shape[-2]
    a = a.reshape(lead + (r, 2 * N_FF_CHUNKS, FF_CHUNK))
    return jnp.moveaxis(a, -2, -3)


def _unchunk_cols(a):
    a = jnp.moveaxis(a, -3, -2)
    return a.reshape(a.shape[:-2] + (2 * D_FF,))


def _bf16_pieces(x):
    hi = x.astype(_BF16).astype(_F32)
    mid = (x - hi).astype(_BF16).astype(_F32)
    lo = (x - hi - mid).astype(_BF16).astype(_F32)
    return hi, mid, lo


def kernel(x_prompt, x_sample, cache_k, cache_v, state_conv, page_table, g_attn, w_qkv, w_o,
           g_ffn, w_up, conv_w, conv_b, w_down, g_final):
    depth = w_qkv.shape[0]
    batch, seq, _ = x_prompt.shape
    dec_b, dec_s, _ = x_sample.shape
    n_pool = cache_k.shape[1]
    n_pages = page_table.shape[1]
    past = n_pages * PAGE_SIZE
    tp = batch * seq
    ts = dec_b * dec_s
    nb = seq // MOBA_BLOCK

    wqkv = w_qkv.astype(_BF16)
    wo = w_o.astype(_BF16)
    wup = w_up.astype(_BF16)
    cw = _chunk_cols(conv_w)
    cb = _chunk_cols(conv_b[:, None, :])
    wdn = w_down.astype(_BF16)
    g_attn2 = g_attn[:, None, :]
    g_ffn2 = g_ffn[:, None, :]
    g_fin2 = g_final[None, :]

    hidx = jnp.arange(1, N_HEADS + 1, dtype=_F32)
    slopes = jnp.exp2(-8.0 * hidx / N_HEADS)
    assert nb <= SLOPE_LANE0 - FLAG_LANE0 and past % MOBA_BLOCK == 0 and dec_s <= PAGE_SIZE
    assert seq % QKV_TM == 0 and seq % FFN_TM == 0 and n_pages % PAGES_PER_STEP == 0
    pieces = jnp.stack(_bf16_pieces(slopes * LOG2E) * 2, axis=1)
    stab = jnp.zeros((N_HEADS, 8, SLAB), _F32).at[:, :, SLOPE_LANE0:SLOPE_LANE0 + 6].set(pieces[:, None, :])
    slope_rows = jnp.repeat(slopes, dec_s)[:, None]
    tok_rows = jnp.tile(jnp.arange(dec_s, dtype=_F32), N_HEADS)[:, None]
    qpos_rows = tok_rows + float(past)
    tok_seq = jnp.tile(jnp.arange(dec_s, dtype=_F32), dec_b)[:, None]

    cache_kt = cache_k.transpose(0, 1, 3, 4, 2).reshape(depth, n_pool, KV_DIM, PAGE_SIZE)
    cache_vt = cache_v.transpose(0, 1, 3, 4, 2).reshape(depth, n_pool, KV_DIM, PAGE_SIZE)

    st = state_conv
    zero = jnp.zeros_like(st[:, :, :1])
    s1 = jnp.concatenate([st[:, :, 1:2], zero, zero, zero][:dec_s], axis=2)
    s2 = jnp.concatenate([st[:, :, 0:1], st[:, :, 1:2], zero, zero][:dec_s], axis=2)
    s1 = _chunk_cols(s1.reshape(depth, ts, 2 * D_FF))
    s2 = _chunk_cols(s2.reshape(depth, ts, 2 * D_FF))

    xp = x_prompt.reshape(tp, D_MODEL)
    xs = x_sample.reshape(ts, D_MODEL)
    ks_l, vs_l, cp_l, cs_l = [], [], [], []
    k_all = jnp.zeros((depth, batch, KV_DIM, seq), _F32)
    v_all = jnp.zeros((depth, batch, KV_DIM, seq), _F32)
    yp = ys = None
    pad_own = jnp.zeros((dec_b, PAGE_SIZE - dec_s, KV_DIM), _BF16)
    for l in range(depth):
        last = l == depth - 1
        gfin = g_fin2 if last else None
        moba = l % 2 == 0
        q, k_all, v_all, kslab, vslab, *km = _rms_qkv(xp, g_attn2[l], wqkv[l], tm=QKV_TM, with_kmean=moba,
                                                      kv_stack=(k_all, v_all), layer=l)
        if moba:
            kmt = km[0].reshape(batch, nb, N_KV_HEADS, HEAD_DIM).transpose(0, 2, 1, 3)
            kmt = jnp.pad(kmt, ((0, 0), (0, 0), (0, SLOPE_LANE0 - FLAG_LANE0 - nb),
                                (0, SLAB - HEAD_DIM)))
            knorm = jnp.sqrt(jnp.max(km[1].reshape(batch, nb, N_KV_HEADS, HEAD_DIM), axis=(1, 3)))
            o = _moba_prompt(knorm.reshape(batch * N_KV_HEADS), stab, q, kslab, vslab, kmt,
                             batch=batch, seq=seq)
        else:
            o = _stick_prompt(q, kslab, vslab, batch=batch, seq=seq)
        outs = _oproj_ffn(xp, o, wo[l], g_ffn2[l], wup[l], cw[l], cb[l], wdn[l],
                          tm=FFN_TM, seq=seq, g_final=gfin)
        xp, cst = outs[0], outs[1]
        if last:
            yp = outs[2]
        cp_l.append(_unchunk_cols(cst[:, :, 6:8, :]))
        qs, ksn, vsn, _, _ = _rms_qkv(xs, g_attn2[l], wqkv[l], tm=ts, with_kmean=False)
        qnat = qs[:, :, :HEAD_DIM].reshape(N_KV_HEADS, GROUP, dec_b, dec_s, HEAD_DIM)
        q_rows = qnat.transpose(2, 1, 3, 0, 4).reshape(dec_b, GROUP * dec_s, KV_DIM)
        k_own = jnp.concatenate([ksn.astype(_BF16).reshape(dec_b, dec_s, KV_DIM), pad_own], axis=1)
        v_own = jnp.concatenate([vsn.astype(_BF16).reshape(dec_b, dec_s, KV_DIM), pad_own], axis=1)
        if moba:
            o_rows = _moba_sample(page_table, slope_rows, qpos_rows, q_rows, k_own, v_own,
                                  cache_kt, cache_vt, layer=l)
        else:
            o_rows = _stick_sample(page_table, tok_rows, q_rows, k_own, v_own, cache_kt, cache_vt, layer=l)
        os_ = o_rows.reshape(dec_b, GROUP, dec_s, N_KV_HEADS, HEAD_DIM).transpose(0, 2, 3, 1, 4)
        outs = _oproj_ffn(xs, os_.reshape(ts, D_MODEL), wo[l], g_ffn2[l], wup[l], cw[l], cb[l], wdn[l],
                          tm=ts, seq=dec_s, sample_state=(s1[l], s2[l], tok_seq), g_final=gfin)
        xs, u_all = outs[0], outs[1]
        if last:
            ys = outs[2]
        ks_l.append(ksn.reshape(dec_b, dec_s, N_KV_HEADS, HEAD_DIM))
        vs_l.append(vsn.reshape(dec_b, dec_s, N_KV_HEADS, HEAD_DIM))
        u_full = _unchunk_cols(u_all).reshape(dec_b, dec_s, 2 * D_FF)
        cs_l.append(u_full[:, dec_s - (CONV_W - 1):, :])
    return (yp.reshape(batch, seq, D_MODEL), ys.reshape(dec_b, dec_s, D_MODEL),
            k_all.reshape(depth, batch, N_KV_HEADS, HEAD_DIM, seq).transpose(0, 1, 4, 2, 3),
            v_all.reshape(depth, batch, N_KV_HEADS, HEAD_DIM, seq).transpose(0, 1, 4, 2, 3),
            jnp.stack(ks_l), jnp.stack(vs_l),
            jnp.stack(cp_l), jnp.stack(cs_l))
```

```python
import functools

import jax
import jax.numpy as jnp
from jax import lax
from jax.experimental import pallas as pl
from jax.experimental.pallas import tpu as pltpu

D_MODEL = 1024
N_HEADS = 16
HEAD_DIM = 64
N_KV_HEADS = 4
GROUP = N_HEADS // N_KV_HEADS
KV_DIM = N_KV_HEADS * HEAD_DIM
SLAB = 2 * HEAD_DIM
MOBA_BLOCK = 256
MOBA_TOPK = 3
D_FF = 2816
CONV_W = 3
PAGE_SIZE = 128
RMS_EPS = 1e-6
NEG_INF = -1e30
SCALE = HEAD_DIM ** -0.5

FF_CHUNK = 256
N_FF_CHUNKS = D_FF // FF_CHUNK
STICK_KB = 128
LOG2E = 1.4426950408889634
STICK_DEAD = -110.0 * LOG2E
MOBA_DEAD = 152.0
NORM_SLACK = 1.04
GATE_LANES = 128
FLAG_LANE0 = HEAD_DIM
SLOPE_LANE0 = 96
PAGES_PER_STEP = 8
MOBA_PAGES_PER_STEP = 16
VMEM_LIMIT = 56 * 1024 * 1024
TQ = 256
STICK_TQ = 256
QKV_TM = 1024
FFN_TM = 512

_BF16 = jnp.bfloat16
_F32 = jnp.float32


def _nt_dot(a, b):
    return lax.dot_general(a, b, (((1,), (1,)), ((), ())), preferred_element_type=_F32)


def _lane_group(shape):
    return lax.broadcasted_iota(jnp.int32, shape, len(shape) - 1) >> 6


def _qkv_body(*refs, tm, stacked):
    if stacked:
        x_ref, g_ref, w_ref, _, _, q_ref, k_ref, v_ref, ks_ref, vs_ref, *km_ref = refs
    else:
        x_ref, g_ref, w_ref, q_ref, k_ref, v_ref, ks_ref, vs_ref, *km_ref = refs
    x = x_ref[...]
    ms = jnp.mean(x * x, axis=-1, keepdims=True)
    h = (x * lax.rsqrt(ms + RMS_EPS)) * g_ref[...]
    qkv = jnp.dot(h.astype(_BF16), w_ref[...], preferred_element_type=_F32)
    for j in range(N_HEADS // 2):
        pair = qkv[:, j * SLAB:(j + 1) * SLAB]
        q_ref[2 * j] = pair
        q_ref[2 * j + 1] = pltpu.roll(pair, HEAD_DIM, 1)
    k = qkv[:, D_MODEL:D_MODEL + KV_DIM]
    v = qkv[:, D_MODEL + KV_DIM:D_MODEL + 2 * KV_DIM]
    k_ref[...] = k.T if stacked else k
    v_ref[...] = v.T if stacked else v
    for j in range(N_KV_HEADS // 2):
        for src, dst in ((k, ks_ref), (v, vs_ref)):
            pair = src[:, j * SLAB:(j + 1) * SLAB]
            dst[2 * j] = pair.astype(_BF16)
            dst[2 * j + 1] = pltpu.roll(pair, HEAD_DIM, 1).astype(_BF16)
    if km_ref:
        li = lax.broadcasted_iota(jnp.int32, (KV_DIM, KV_DIM), 0) >> 6
        lj = lax.broadcasted_iota(jnp.int32, (KV_DIM, KV_DIM), 1) >> 6
        k2 = jnp.dot((k * k).astype(_BF16), jnp.where(li == lj, 1.0, 0.0).astype(_BF16),
                     preferred_element_type=_F32)
        for j in range(tm // MOBA_BLOCK):
            blk = slice(j * MOBA_BLOCK, (j + 1) * MOBA_BLOCK)
            km_ref[0][j] = jnp.mean(k[blk], axis=0, keepdims=True)
            km_ref[1][j] = jnp.max(k2[blk], axis=0, keepdims=True)


def _rms_qkv(x, g, w, *, tm, with_kmean, kv_stack=None, layer=0):
    t = x.shape[0]
    stacked = kv_stack is not None
    if stacked:
        seq = kv_stack[0].shape[3]
        tps = seq // tm
        kv_shape = [jax.ShapeDtypeStruct(kv_stack[0].shape, _F32)] * 2
        row = pl.BlockSpec((None, None, KV_DIM, tm), lambda i: (layer, i // tps, 0, i % tps))
    else:
        kv_shape = [jax.ShapeDtypeStruct((t, KV_DIM), _F32)] * 2
        row = pl.BlockSpec((tm, KV_DIM), lambda i: (i, 0))
    out_shape = [jax.ShapeDtypeStruct((N_HEADS, t, SLAB), _F32)] + kv_shape + [
        jax.ShapeDtypeStruct((N_KV_HEADS, t, SLAB), _BF16),
        jax.ShapeDtypeStruct((N_KV_HEADS, t, SLAB), _BF16),
    ]
    slab = pl.BlockSpec((N_KV_HEADS, tm, SLAB), lambda i: (0, i, 0))
    out_specs = [pl.BlockSpec((N_HEADS, tm, SLAB), lambda i: (0, i, 0)), row, row, slab, slab]
    if with_kmean:
        for _ in range(2):
            out_shape.append(jax.ShapeDtypeStruct((t // MOBA_BLOCK, 1, KV_DIM), _F32))
            out_specs.append(pl.BlockSpec((tm // MOBA_BLOCK, 1, KV_DIM), lambda i: (i, 0, 0)))
    in_specs = [
        pl.BlockSpec((tm, D_MODEL), lambda i: (i, 0)),
        pl.BlockSpec((1, D_MODEL), lambda i: (0, 0)),
        pl.BlockSpec(w.shape, lambda i: (0, 0)),
    ]
    args = [x, g, w]
    aliases = {}
    if stacked:
        in_specs += [pl.BlockSpec(memory_space=pl.ANY)] * 2
        args += list(kv_stack)
        aliases = {3: 1, 4: 2}
    return pl.pallas_call(
        functools.partial(_qkv_body, tm=tm, stacked=stacked),
        grid=(t // tm,),
        in_specs=in_specs,
        out_specs=out_specs,
        out_shape=out_shape,
        input_output_aliases=aliases,
        compiler_params=pltpu.CompilerParams(dimension_semantics=("arbitrary",),
                                             vmem_limit_bytes=VMEM_LIMIT),
        name="rms_qkv",
    )(*args)


def _top3_select(gate, eligible, axis):
    idx = lax.broadcasted_iota(jnp.int32, gate.shape, axis).astype(_F32)
    neg = jnp.float32(-jnp.inf)
    gm = jnp.where(eligible, gate, neg)
    sel = jnp.zeros(gate.shape, _F32)
    for _ in range(MOBA_TOPK):
        mx = jnp.max(gm, axis=axis, keepdims=True)
        first = jnp.min(jnp.where(gm == mx, idx, jnp.float32(1e9)), axis=axis, keepdims=True)
        pick = jnp.logical_and(idx == first, mx > neg)
        sel = jnp.where(pick, 1.0, sel)
        gm = jnp.where(pick, neg, gm)
    return sel


def _stick_tri():
    j = lax.broadcasted_iota(jnp.int32, (2 * STICK_KB, 2 * STICK_KB), 0) & (STICK_KB - 1)
    s = lax.broadcasted_iota(jnp.int32, (2 * STICK_KB, 2 * STICK_KB), 1)
    return jnp.where(jnp.logical_or(s >= STICK_KB, j > s), 1.0, 0.0).astype(_BF16)


def _stick_chunk(z, causal, carry, tri, vc_nt=None, vc=None):
    nz = -z
    e = jnp.exp2(jnp.minimum(z, nz))
    log_keep = jnp.minimum(nz, 0.0) - jnp.log2(1.0 + e)
    log_sig = z + log_keep
    if causal is not None:
        log_keep = jnp.where(causal, log_keep, 0.0)
    hi = log_keep.astype(_BF16)
    lo = (log_keep - hi.astype(_F32)).astype(_BF16)
    ct = jnp.dot(jnp.concatenate([hi, lo], axis=1), tri, preferred_element_type=_F32)
    log_after = carry + ct[:, :STICK_KB]
    a = jnp.exp2(log_sig + log_after)
    if causal is not None:
        a = jnp.where(causal, a, 0.0)
    a = a.astype(_BF16)
    pv = _nt_dot(a, vc_nt) if vc is None else jnp.dot(a, vc, preferred_element_type=_F32)
    return pv, carry + ct[:, STICK_KB:]


def _pair_heads(acc_ref, o_ref, normalise):
    low = lax.broadcasted_iota(jnp.int32, acc_ref.shape[1:], 1) < HEAD_DIM
    for j in range(N_HEADS // 2):
        a0 = acc_ref[2 * j]
        a1 = acc_ref[2 * j + 1]
        out = jnp.where(low, a0, pltpu.roll(a1, HEAD_DIM, 1))
        if normalise:
            out = out / jnp.where(low, pltpu.roll(a0, HEAD_DIM, 1), a1)
        o_ref[:, j * SLAB:(j + 1) * SLAB] = out.astype(o_ref.dtype)


def _moba_prompt_body(kmax_ref, stab_ref, q_ref, k_ref, v_ref, km_ref, o_ref, qa_ref, m_ref, acc_ref, *, tq):
    i = pl.program_id(1)
    rows = GROUP * tq
    n_flags = SLOPE_LANE0 - FLAG_LANE0
    krow = lax.broadcasted_iota(jnp.int32, (MOBA_BLOCK, SLAB), 0)
    klane = lax.broadcasted_iota(jnp.int32, (MOBA_BLOCK, SLAB), 1)
    klow = klane < HEAD_DIM
    one = jnp.ones((MOBA_BLOCK, SLAB), _BF16)

    def key_aux(block_offset, c):
        f = jnp.where(klane == FLAG_LANE0 + c, NEG_INF, 0.0)
        f = jnp.where(jnp.logical_and(klane >= SLOPE_LANE0, klane < SLOPE_LANE0 + 3), block_offset, f)
        f = jnp.where(jnp.logical_and(klane >= SLOPE_LANE0 + 3, klane < SLOPE_LANE0 + 6),
                      krow.astype(_F32), f)
        return f.astype(_BF16)

    low4 = lax.broadcasted_iota(jnp.int32, (rows, SLAB), 1) < HEAD_DIM
    eligible4 = lax.broadcasted_iota(jnp.int32, (n_flags, rows), 0) < i

    def kv_head(n, _):
        hs = pl.ds(n * GROUP, GROUP)
        q4 = q_ref[hs].reshape(rows, SLAB)
        gate_t = lax.dot_general(km_ref[n], q4, (((1,), (1,)), ((), ())),
                                 precision=lax.Precision.HIGHEST, preferred_element_type=_F32)
        sel_t = _top3_select(gate_t, eligible4, 0)
        notsel_t = jnp.where(eligible4, 1.0 - sel_t, 0.0)
        notsel = jnp.concatenate([jnp.zeros((FLAG_LANE0, rows), _F32), notsel_t,
                                  jnp.zeros((GATE_LANES - FLAG_LANE0 - n_flags, rows), _F32)],
                                 axis=0).T
        slope4 = jnp.concatenate([jnp.broadcast_to(stab_ref[n * GROUP + g][0:1, :], (tq, SLAB))
                                  for g in range(GROUP)], axis=0)
        q2 = q4 * (SCALE * LOG2E)
        qa_ref[hs] = jnp.where(low4, q2, notsel + slope4).astype(_BF16).reshape(GROUP, tq, SLAB)
        qn = jnp.sqrt(jnp.sum(jnp.where(low4, q2 * q2, 0.0), axis=1, keepdims=True))
        m_ref[hs] = jnp.broadcast_to(qn, (rows, SLAB)).reshape(GROUP, tq, SLAB)
        return 0

    lax.fori_loop(0, N_KV_HEADS, kv_head, 0)

    off = pl.multiple_of(i * MOBA_BLOCK, MOBA_BLOCK)
    f_own = key_aux(0.0, i)
    qrow = lax.broadcasted_iota(jnp.int32, (rows, MOBA_BLOCK), 0) & (tq - 1)
    kcol = lax.broadcasted_iota(jnp.int32, (rows, MOBA_BLOCK), 1)
    causal = qrow >= kcol
    keep = []
    for n in range(N_KV_HEADS):
        hs = pl.ds(n * GROUP, GROUP)
        kk = jnp.where(klow, k_ref[n, pl.ds(off, MOBA_BLOCK), :], f_own)
        vv = jnp.where(klow, v_ref[n, pl.ds(off, MOBA_BLOCK), :], one)
        s = _nt_dot(qa_ref[hs].reshape(rows, SLAB), kk)
        s = jnp.where(causal, s, NEG_INF)
        m0 = jnp.max(s, axis=1, keepdims=True)
        p = jnp.exp2(s - m0)
        qn = m_ref[hs].reshape(rows, SLAB)[:, 0:1]
        m_ref[hs] = jnp.broadcast_to(m0, (rows, SLAB)).reshape(GROUP, tq, SLAB)
        acc_ref[hs] = jnp.dot(p.astype(_BF16), vv, preferred_element_type=_F32).reshape(GROUP, tq, SLAB)
        slope_min = 2.0 ** (-8.0 * (n * GROUP + GROUP) / N_HEADS) * LOG2E * 0.999
        reach = (qn * (kmax_ref[pl.program_id(0) * N_KV_HEADS + n] * NORM_SLACK) + MOBA_DEAD - m0)
        blocks = jnp.max(reach) * (1.0 / (MOBA_BLOCK * slope_min)) - 1.0 / MOBA_BLOCK
        k_n = jnp.int32(0)
        for d in range(n_flags):
            k_n = k_n + (blocks >= float(d)).astype(jnp.int32)
        keep.append(k_n if n == 0 else jnp.maximum(k_n, keep[-1]))

    def past(c, _, groups):
        offc = pl.multiple_of(c * MOBA_BLOCK, MOBA_BLOCK)
        f_c = key_aux(((c - i) * MOBA_BLOCK).astype(_F32), c)
        for n in groups:
            hs = pl.ds(n * GROUP, GROUP)
            kk = jnp.where(klow, k_ref[n, pl.ds(offc, MOBA_BLOCK), :], f_c)
            vv = jnp.where(klow, v_ref[n, pl.ds(offc, MOBA_BLOCK), :], one)
            s = _nt_dot(qa_ref[hs].reshape(rows, SLAB), kk)
            m_old = m_ref[hs].reshape(rows, SLAB)
            m_new = jnp.maximum(m_old, jnp.max(s, axis=1, keepdims=True))
            alpha = jnp.exp2(m_old - m_new)
            p = jnp.exp2(s - jnp.concatenate([m_new, m_new], axis=1))
            pv = jnp.dot(p.astype(_BF16), vv, preferred_element_type=_F32)
            m_ref[hs] = m_new.reshape(GROUP, tq, SLAB)
            acc_ref[hs] = (alpha * acc_ref[hs].reshape(rows, SLAB) + pv).reshape(GROUP, tq, SLAB)
        return 0

    start = [jnp.maximum(i - k_n, 0) for k_n in keep]
    for n in range(N_KV_HEADS - 1, -1, -1):
        stop = start[n - 1] if n > 0 else i
        body = functools.partial(past, groups=tuple(range(n, N_KV_HEADS)))
        if n == 0:
            lax.fori_loop(start[n], stop, body, 0)
            continue
        pairs = (stop - start[n]) // 2

        def two(t, _, body=body, first=start[n]):
            body(first + 2 * t, 0)
            body(first + 2 * t + 1, 0)
            return 0

        lax.fori_loop(0, pairs, two, 0)
        lax.fori_loop(start[n] + 2 * pairs, stop, body, 0)
    _pair_heads(acc_ref, o_ref, normalise=True)


def _moba_prompt(kmax, stab, q, ks, vs, kmtab, *, batch, seq):
    tq = TQ
    nq = seq // tq
    return pl.pallas_call(
        functools.partial(_moba_prompt_body, tq=tq),
        grid=(batch, nq),
        in_specs=[
            pl.BlockSpec(memory_space=pltpu.SMEM),
            pl.BlockSpec(stab.shape, lambda b, i: (0, 0, 0)),
            pl.BlockSpec((N_HEADS, tq, SLAB), lambda b, i: (0, b * nq + i, 0)),
            pl.BlockSpec((N_KV_HEADS, seq, SLAB), lambda b, i: (0, b, 0)),
            pl.BlockSpec((N_KV_HEADS, seq, SLAB), lambda b, i: (0, b, 0)),
            pl.BlockSpec((None,) + kmtab.shape[1:], lambda b, i: (b, 0, 0, 0)),
        ],
        out_specs=pl.BlockSpec((tq, D_MODEL), lambda b, i: (b * nq + i, 0)),
        out_shape=jax.ShapeDtypeStruct((batch * seq, D_MODEL), _BF16),
        scratch_shapes=[pltpu.VMEM((N_HEADS, tq, SLAB), _BF16),
                        pltpu.VMEM((N_HEADS, tq, SLAB), _F32),
                        pltpu.VMEM((N_HEADS, tq, SLAB), _F32)],
        compiler_params=pltpu.CompilerParams(dimension_semantics=("arbitrary", "arbitrary"),
                                             vmem_limit_bytes=VMEM_LIMIT),
        name="moba_prompt",
    )(kmax, stab, q, ks, vs, kmtab)


def _stick_prompt_body(q_ref, k_ref, v_ref, o_ref, qa_ref, acc_ref, carry_ref, *, tq):
    i = pl.program_id(1)
    rows = GROUP * tq
    low = lax.broadcasted_iota(jnp.int32, q_ref.shape, 2) < HEAD_DIM
    qa_ref[...] = jnp.where(low, q_ref[...] * (SCALE * LOG2E), 0.0).astype(_BF16)
    acc_ref[...] = jnp.zeros(acc_ref.shape, _F32)
    carry_ref[...] = jnp.zeros(carry_ref.shape, _F32)
    tri = _stick_tri()
    qrow = lax.broadcasted_iota(jnp.int32, (rows, STICK_KB), 0) & (tq - 1)
    kcol = lax.broadcasted_iota(jnp.int32, (rows, STICK_KB), 1)
    rel = qrow - kcol
    chunks_per_tile = tq // STICK_KB

    def chunk(c, masked):
        off = pl.multiple_of(c * STICK_KB, STICK_KB)
        causal = (rel + (i * tq - c * STICK_KB) > 0) if masked else None
        top = jnp.float32(-jnp.inf)
        for n in range(N_KV_HEADS):
            hs = pl.ds(n * GROUP, GROUP)
            z = _nt_dot(qa_ref[hs].reshape(rows, SLAB), k_ref[n, pl.ds(off, STICK_KB), :])
            pv, carry = _stick_chunk(z, causal, carry_ref[hs].reshape(rows, STICK_KB), tri,
                                     vc=v_ref[n, pl.ds(off, STICK_KB), :])
            acc_ref[hs] += pv.reshape(GROUP, tq, SLAB)
            carry_ref[hs] = carry.reshape(GROUP, tq, STICK_KB)
            top = jnp.maximum(top, jnp.max(carry))
        return top

    top = jnp.float32(0.0)
    for d in range(chunks_per_tile):
        top = chunk((i + 1) * chunks_per_tile - 1 - d, True)

    def cond(state):
        c, alive = state
        return jnp.logical_and(c >= 0, alive > 0)

    def step(state):
        c, _ = state
        return c - 1, (chunk(c, False) > STICK_DEAD).astype(jnp.int32)

    lax.while_loop(cond, step, (i * chunks_per_tile - 1, (top > STICK_DEAD).astype(jnp.int32)))
    _pair_heads(acc_ref, o_ref, normalise=False)


def _stick_prompt(q, ks, vs, *, batch, seq):
    tq = STICK_TQ
    nq = seq // tq
    return pl.pallas_call(
        functools.partial(_stick_prompt_body, tq=tq),
        grid=(batch, nq),
        in_specs=[
            pl.BlockSpec((N_HEADS, tq, SLAB), lambda b, i: (0, b * nq + i, 0)),
            pl.BlockSpec((N_KV_HEADS, seq, SLAB), lambda b, i: (0, b, 0)),
            pl.BlockSpec((N_KV_HEADS, seq, SLAB), lambda b, i: (0, b, 0)),
        ],
        out_specs=pl.BlockSpec((tq, D_MODEL), lambda b, i: (b * nq + i, 0)),
        out_shape=jax.ShapeDtypeStruct((batch * seq, D_MODEL), _BF16),
        scratch_shapes=[pltpu.VMEM((N_HEADS, tq, SLAB), _BF16),
                        pltpu.VMEM((N_HEADS, tq, SLAB), _F32),
                        pltpu.VMEM((N_HEADS, tq, STICK_KB), _F32)],
        compiler_params=pltpu.CompilerParams(dimension_semantics=("arbitrary", "arbitrary"),
                                             vmem_limit_bytes=VMEM_LIMIT),
        name="stick_prompt",
    )(q, ks, vs)


def _sample_rows(q_ref, scale):
    q = q_ref[0] * scale
    grp = _lane_group(q.shape)
    return jnp.concatenate([jnp.where(grp == n, q, 0.0) for n in range(N_KV_HEADS)], axis=0)


def _sample_out(acc):
    r = acc.shape[0] // N_KV_HEADS
    grp = _lane_group((r, KV_DIM))
    out = jnp.zeros((r, KV_DIM), _F32)
    for n in range(N_KV_HEADS):
        out = jnp.where(grp == n, acc[n * r:(n + 1) * r], out)
    return out


def _moba_sample_body(pt_ref, slope_ref, qpos_ref, q_ref, kown_ref, vown_ref, *rest, n_steps, pps):
    kp = rest[:pps]
    vp = rest[pps:2 * pps]
    o_ref, oall_ref, mtab_ref, ltab_ref, kmt_ref = rest[2 * pps:]
    sp = pl.program_id(1)
    qm = _sample_rows(q_ref, SCALE)
    qb = qm.astype(_BF16)
    rows = qm.shape[0]
    slope = slope_ref[...]
    qpos = qpos_ref[...]
    gcol = lax.broadcasted_iota(jnp.int32, (rows, GATE_LANES), 1)
    kmcol = lax.broadcasted_iota(jnp.int32, (KV_DIM, GATE_LANES), 1)
    kcol = lax.broadcasted_iota(jnp.int32, (rows, MOBA_BLOCK), 1).astype(_F32)
    pages_per_block = MOBA_BLOCK // PAGE_SIZE
    blocks_per_step = pps // pages_per_block

    @pl.when(sp == 0)
    def _():
        for tab in (mtab_ref, ltab_ref, kmt_ref):
            tab[...] = jnp.zeros(tab.shape, _F32)

    kmt, mtab, ltab = kmt_ref[...], mtab_ref[...], ltab_ref[...]
    for jb in range(blocks_per_step):
        m = sp * blocks_per_step + jb
        kblk = jnp.concatenate([kp[jb * pages_per_block + u][...] for u in range(pages_per_block)], axis=1)
        vblk = jnp.concatenate([vp[jb * pages_per_block + u][...] for u in range(pages_per_block)], axis=1)
        ksum = jnp.sum(kblk, axis=1, keepdims=True)
        kmt = jnp.where(kmcol == m, ksum * (1.0 / MOBA_BLOCK), kmt)
        kpos = kcol + (m * MOBA_BLOCK).astype(_F32)
        s = jnp.dot(qb, kblk.astype(_BF16), preferred_element_type=_F32) - slope * (qpos - kpos)
        mb = jnp.max(s, axis=1, keepdims=True)
        p = jnp.exp(s - mb)
        oall_ref[m] = _nt_dot(p.astype(_BF16), vblk.astype(_BF16))
        here = gcol == m
        mtab = jnp.where(here, mb, mtab)
        ltab = jnp.where(here, jnp.sum(p, axis=1, keepdims=True), ltab)
    kmt_ref[...] = kmt
    mtab_ref[...] = mtab
    ltab_ref[...] = ltab

    @pl.when(sp == n_steps - 1)
    def _():
        n_blocks = n_steps * blocks_per_step
        ocol = lax.broadcasted_iota(jnp.int32, (rows, PAGE_SIZE), 1).astype(_F32)
        dist = (qpos - (n_blocks * MOBA_BLOCK)) - ocol
        s = _nt_dot(qb, kown_ref[0]) - slope * dist
        s = jnp.where(dist >= 0.0, s, NEG_INF)
        m_own = jnp.max(s, axis=1, keepdims=True)
        p = jnp.exp(s - m_own)
        l_own = jnp.sum(p, axis=1, keepdims=True)
        o_own = jnp.dot(p.astype(_BF16), vown_ref[0], preferred_element_type=_F32)

        gate = jnp.dot(qm, kmt_ref[...], precision=lax.Precision.HIGHEST, preferred_element_type=_F32)
        picked = _top3_select(gate, gcol < n_blocks, 1) > 0.0
        mtab = mtab_ref[...]
        m_all = jnp.maximum(m_own, jnp.max(jnp.where(picked, mtab, NEG_INF), axis=1, keepdims=True))
        w = jnp.where(picked, jnp.exp(mtab - m_all), 0.0)
        w_own = jnp.exp(m_own - m_all)
        l_all = w_own * l_own + jnp.sum(w * ltab_ref[...], axis=1, keepdims=True)

        def merge(mi, acc):
            wm = jnp.sum(jnp.where(gcol == mi, w, 0.0), axis=1, keepdims=True)
            return acc + wm * oall_ref[mi]

        acc = lax.fori_loop(0, n_blocks, merge, w_own * o_own)
        o_ref[0] = _sample_out(acc / l_all).astype(o_ref.dtype)


def _page_specs(layer, page_of, pps=PAGES_PER_STEP):
    def spec(j):
        return pl.BlockSpec((None, None, KV_DIM, PAGE_SIZE),
                            lambda b, s, pt, *_: (layer, pt[b, page_of(s, j)], 0, 0))
    return [spec(j) for j in range(pps)] * 2


def _moba_sample(page_table, slope_rows, qpos_rows, q_rows, k_own, v_own, cache_kt, cache_vt, *, layer):
    dec_b, n_pages = page_table.shape
    pps = MOBA_PAGES_PER_STEP if n_pages % MOBA_PAGES_PER_STEP == 0 else PAGES_PER_STEP
    n_steps = n_pages // pps
    rows = q_rows.shape[1] * N_KV_HEADS
    n_blocks = n_pages * PAGE_SIZE // MOBA_BLOCK
    grid_spec = pltpu.PrefetchScalarGridSpec(
        num_scalar_prefetch=1,
        grid=(dec_b, n_steps),
        in_specs=[
            pl.BlockSpec((rows, 1), lambda b, s, pt: (0, 0)),
            pl.BlockSpec((rows, 1), lambda b, s, pt: (0, 0)),
            pl.BlockSpec((1, q_rows.shape[1], KV_DIM), lambda b, s, pt: (b, 0, 0)),
            pl.BlockSpec((1, PAGE_SIZE, KV_DIM), lambda b, s, pt: (b, 0, 0)),
            pl.BlockSpec((1, PAGE_SIZE, KV_DIM), lambda b, s, pt: (b, 0, 0)),
        ] + _page_specs(layer, lambda s, j: s * pps + j, pps),
        out_specs=pl.BlockSpec((1, q_rows.shape[1], KV_DIM), lambda b, s, pt: (b, 0, 0)),
        scratch_shapes=[pltpu.VMEM((n_blocks, rows, KV_DIM), _F32),
                        pltpu.VMEM((rows, GATE_LANES), _F32),
                        pltpu.VMEM((rows, GATE_LANES), _F32),
                        pltpu.VMEM((KV_DIM, GATE_LANES), _F32)],
    )
    return pl.pallas_call(
        functools.partial(_moba_sample_body, n_steps=n_steps, pps=pps),
        grid_spec=grid_spec,
        out_shape=jax.ShapeDtypeStruct(q_rows.shape, _BF16),
        compiler_params=pltpu.CompilerParams(dimension_semantics=("arbitrary", "arbitrary"),
                                             vmem_limit_bytes=VMEM_LIMIT),
        name="moba_sample",
    )(page_table, slope_rows, qpos_rows, q_rows, k_own, v_own,
      *([cache_kt] * pps), *([cache_vt] * pps))


def _stick_pages(qb, kp, vp, tri, acc_ref, carry_ref, alive_ref):
    for j in range(PAGES_PER_STEP):
        @pl.when(alive_ref[0] > 0)
        def _():
            z = jnp.dot(qb, kp[j][...].astype(_BF16), preferred_element_type=_F32)
            pv, carry = _stick_chunk(z, None, carry_ref[...], tri, vc_nt=vp[j][...].astype(_BF16))
            acc_ref[...] += pv
            carry_ref[...] = carry
            alive_ref[0] = (jnp.max(carry) > STICK_DEAD).astype(jnp.int32)


def _stick_sample_first_body(pt_ref, tok_ref, q_ref, kown_ref, vown_ref, *rest):
    kp = rest[:PAGES_PER_STEP]
    vp = rest[PAGES_PER_STEP:2 * PAGES_PER_STEP]
    o_ref, acc_ref, carry_ref, alive_ref = rest[2 * PAGES_PER_STEP:]
    qb = _sample_rows(q_ref, SCALE * LOG2E).astype(_BF16)
    rows = qb.shape[0]
    tri = _stick_tri()
    kcol = lax.broadcasted_iota(jnp.int32, (rows, STICK_KB), 1).astype(_F32)
    causal = kcol < tok_ref[...]
    z = _nt_dot(qb, kown_ref[0])
    pv, carry = _stick_chunk(z, causal, jnp.zeros((rows, STICK_KB), _F32), tri, vc=vown_ref[0])
    acc_ref[0] = pv
    carry_ref[0] = carry
    alive_ref[0] = (jnp.max(carry) > STICK_DEAD).astype(jnp.int32)
    _stick_pages(qb, kp, vp, tri, acc_ref.at[0], carry_ref.at[0], alive_ref)
    o_ref[0] = _sample_out(acc_ref[0]).astype(o_ref.dtype)


def _stick_sample_rest_body(pt_ref, live_ref, q_ref, acc_in_ref, carry_in_ref, *rest, n_steps):
    kp = rest[:PAGES_PER_STEP]
    vp = rest[PAGES_PER_STEP:2 * PAGES_PER_STEP]
    o_ref, acc_ref, carry_ref, alive_ref = rest[2 * PAGES_PER_STEP:]
    sp = pl.program_id(1)
    qb = _sample_rows(q_ref, SCALE * LOG2E).astype(_BF16)

    @pl.when(sp == 0)
    def _():
        acc_ref[...] = acc_in_ref[0]
        carry_ref[...] = carry_in_ref[0]
        alive_ref[0] = live_ref[pl.program_id(0)]

    _stick_pages(qb, kp, vp, _stick_tri(), acc_ref, carry_ref, alive_ref)

    @pl.when(sp == n_steps - 1)
    def _():
        o_ref[0] = _sample_out(acc_ref[...]).astype(o_ref.dtype)


def _stick_sample(page_table, tok_rows, q_rows, k_own, v_own, cache_kt, cache_vt, *, layer):
    dec_b, n_pages = page_table.shape
    n_steps = n_pages // PAGES_PER_STEP
    qr = q_rows.shape[1]
    rows = qr * N_KV_HEADS
    per_b = lambda b, s, *_: (b, 0, 0)
    params = pltpu.CompilerParams(dimension_semantics=("arbitrary", "arbitrary"),
                                  vmem_limit_bytes=VMEM_LIMIT)
    pages = [cache_kt] * PAGES_PER_STEP + [cache_vt] * PAGES_PER_STEP
    o_rows, acc, carry = pl.pallas_call(
        _stick_sample_first_body,
        grid_spec=pltpu.PrefetchScalarGridSpec(
            num_scalar_prefetch=1,
            grid=(dec_b, 1),
            in_specs=[
                pl.BlockSpec((rows, 1), lambda b, s, *_: (0, 0)),
                pl.BlockSpec((1, qr, KV_DIM), per_b),
                pl.BlockSpec((1, PAGE_SIZE, KV_DIM), per_b),
                pl.BlockSpec((1, PAGE_SIZE, KV_DIM), per_b),
            ] + _page_specs(layer, lambda s, j: n_pages - 1 - j),
            out_specs=[pl.BlockSpec((1, qr, KV_DIM), per_b),
                       pl.BlockSpec((1, rows, KV_DIM), per_b),
                       pl.BlockSpec((1, rows, STICK_KB), per_b)],
            scratch_shapes=[pltpu.SMEM((1,), jnp.int32)],
        ),
        out_shape=[jax.ShapeDtypeStruct(q_rows.shape, _BF16),
                   jax.ShapeDtypeStruct((dec_b, rows, KV_DIM), _F32),
                   jax.ShapeDtypeStruct((dec_b, rows, STICK_KB), _F32)],
        compiler_params=params,
        name="stick_sample_first",
    )(page_table, tok_rows, q_rows, k_own, v_own, *pages)
    if n_steps == 1:
        return o_rows
    live = (jnp.max(carry, axis=(1, 2)) > STICK_DEAD).astype(jnp.int32)
    pt_rest = jnp.where(live[:, None] > 0, page_table, page_table[0, 0])
    rest = pl.pallas_call(
        functools.partial(_stick_sample_rest_body, n_steps=n_steps - 1),
        grid_spec=pltpu.PrefetchScalarGridSpec(
            num_scalar_prefetch=2,
            grid=(dec_b, n_steps - 1),
            in_specs=[
                pl.BlockSpec((1, qr, KV_DIM), per_b),
                pl.BlockSpec((1, rows, KV_DIM), per_b),
                pl.BlockSpec((1, rows, STICK_KB), per_b),
            ] + _page_specs(layer, lambda s, j: n_pages - 1 - ((s + 1) * PAGES_PER_STEP + j)),
            out_specs=pl.BlockSpec((1, qr, KV_DIM), per_b),
            scratch_shapes=[pltpu.VMEM((rows, KV_DIM), _F32),
                            pltpu.VMEM((rows, STICK_KB), _F32),
                            pltpu.SMEM((1,), jnp.int32)],
        ),
        out_shape=jax.ShapeDtypeStruct(q_rows.shape, _BF16),
        compiler_params=params,
        name="stick_sample_rest",
    )
    return lax.cond(jnp.max(live) > 0,
                    lambda: rest(pt_rest, live, q_rows, acc, carry, *pages),
                    lambda: o_rows)


def _ffn_body(*refs, tm, tiles_per_seq, sample, final):
    it = iter(refs)
    x_ref, o_ref, wo_ref, g_ref, wup_ref, cw_ref, cb_ref, wdn_ref = (next(it) for _ in range(8))
    s1_ref = s2_ref = tok_ref = gfin_ref = None
    if sample:
        s1_ref, s2_ref, tok_ref = next(it), next(it), next(it)
    if final:
        gfin_ref = next(it)
    xo_ref, u_ref = next(it), next(it)
    y_ref = next(it) if final else None
    ubuf, carry, abuf = next(it), next(it), next(it)

    i = pl.program_id(0)
    x1 = x_ref[...] + jnp.dot(o_ref[...], wo_ref[...], preferred_element_type=_F32)
    ms = jnp.mean(x1 * x1, axis=-1, keepdims=True)
    h = ((x1 * lax.rsqrt(ms + RMS_EPS)) * g_ref[...]).astype(_BF16)
    xo_ref[...] = x1

    if not sample:
        @pl.when(i % tiles_per_seq == 0)
        def _():
            carry[...] = jnp.zeros(carry.shape, _F32)
    ubuf[:, 0:8, :] = jnp.zeros((ubuf.shape[0], 8, FF_CHUNK), _F32)

    def conv(jj, slot):
        u = jnp.dot(h, wup_ref[:, jj * FF_CHUNK:(jj + 1) * FF_CHUNK], preferred_element_type=_F32)
        buf = ubuf.at[slot]
        buf[8:8 + tm, :] = u
        if sample:
            u_ref[jj] = u
            tok = tok_ref[...]
            u1 = jnp.where(tok < 1.0, s1_ref[jj], buf[7:7 + tm, :])
            u2 = jnp.where(tok < 2.0, s2_ref[jj], buf[6:6 + tm, :])
        else:
            buf[6:8, :] = carry[jj, 6:8, :]
            u1 = buf[7:7 + tm, :]
            u2 = buf[6:6 + tm, :]
            carry[jj, 6:8, :] = u[tm - 2:tm, :]
        w = cw_ref[jj]
        return cb_ref[jj] + w[0:1, :] * u2 + w[1:2, :] * u1 + w[2:3, :] * u

    for j in range(N_FF_CHUNKS):
        cg = conv(j, 2 * (j % 2))
        cv = conv(j + N_FF_CHUNKS, 2 * (j % 2) + 1)
        act = (cg / (1.0 + jnp.exp(-cg))) * cv
        abuf[:, j * FF_CHUNK:(j + 1) * FF_CHUNK] = act.astype(_BF16)
    xo_ref[...] += jnp.dot(abuf[...], wdn_ref[...], preferred_element_type=_F32)
    if not sample:
        u_ref[...] = carry[...]
    if final:
        xo = xo_ref[...]
        ms2 = jnp.mean(xo * xo, axis=-1, keepdims=True)
        y_ref[...] = (xo * lax.rsqrt(ms2 + RMS_EPS)) * gfin_ref[...]


def _oproj_ffn(x, o, wo, g, wup, cw, cb, wdn, *, tm, seq, sample_state=None, g_final=None):
    t = x.shape[0]
    sample = sample_state is not None
    final = g_final is not None
    n_tiles = t // tm
    tiles_per_seq = max(seq // tm, 1)
    const2 = lambda i: (0, 0)
    const3 = lambda i: (0, 0, 0)
    once = pl.Buffered(1)
    in_specs = [
        pl.BlockSpec((tm, D_MODEL), lambda i: (i, 0)),
        pl.BlockSpec((tm, D_MODEL), lambda i: (i, 0)),
        pl.BlockSpec(wo.shape, const2, pipeline_mode=once),
        pl.BlockSpec((1, D_MODEL), const2),
        pl.BlockSpec(wup.shape, const2, pipeline_mode=once),
        pl.BlockSpec(cw.shape, const3),
        pl.BlockSpec(cb.shape, const3),
        pl.BlockSpec(wdn.shape, const2, pipeline_mode=once),
    ]
    args = [x, o, wo, g, wup, cw, cb, wdn]
    if sample:
        s1, s2, tok = sample_state
        in_specs += [pl.BlockSpec(s1.shape, const3), pl.BlockSpec(s2.shape, const3),
                     pl.BlockSpec(tok.shape, const2)]
        args += [s1, s2, tok]
    if final:
        in_specs.append(pl.BlockSpec((1, D_MODEL), const2))
        args.append(g_final)
    out_shape = [jax.ShapeDtypeStruct((t, D_MODEL), _F32)]
    out_specs = [pl.BlockSpec((tm, D_MODEL), lambda i: (i, 0))]
    if sample:
        out_shape.append(jax.ShapeDtypeStruct((2 * N_FF_CHUNKS, t, FF_CHUNK), _F32))
        out_specs.append(pl.BlockSpec((2 * N_FF_CHUNKS, tm, FF_CHUNK), lambda i: (0, i, 0)))
    else:
        n_seq = t // seq
        out_shape.append(jax.ShapeDtypeStruct((n_seq, 2 * N_FF_CHUNKS, 8, FF_CHUNK), _F32))
        out_specs.append(pl.BlockSpec((None, 2 * N_FF_CHUNKS, 8, FF_CHUNK),
                                      lambda i: (i // tiles_per_seq, 0, 0, 0)))
    if final:
        out_shape.append(jax.ShapeDtypeStruct((t, D_MODEL), _F32))
        out_specs.append(pl.BlockSpec((tm, D_MODEL), lambda i: (i, 0)))
    return pl.pallas_call(
        functools.partial(_ffn_body, tm=tm, tiles_per_seq=tiles_per_seq, sample=sample, final=final),
        grid=(n_tiles,),
        in_specs=in_specs,
        out_specs=out_specs,
        out_shape=out_shape,
        scratch_shapes=[pltpu.VMEM((4, tm + 8, FF_CHUNK), _F32),
                        pltpu.VMEM((2 * N_FF_CHUNKS, 8, FF_CHUNK), _F32),
                        pltpu.VMEM((tm, D_FF), _BF16)],
        compiler_params=pltpu.CompilerParams(dimension_semantics=("arbitrary",),
                                             vmem_limit_bytes=VMEM_LIMIT),
        name="oproj_ffn_sample" if sample else "oproj_ffn",
    )(*args)


def _chunk_cols(a):
    lead = a.shape[:-2]
    r = a.shape[-2]
    a = a.reshape(lead + (r, 2 * N_FF_CHUNKS, FF_CHUNK))
    return jnp.moveaxis(a, -2, -3)


def _unchunk_cols(a):
    a = jnp.moveaxis(a, -3, -2)
    return a.reshape(a.shape[:-2] + (2 * D_FF,))


def _bf16_pieces(x):
    hi = x.astype(_BF16).astype(_F32)
    mid = (x - hi).astype(_BF16).astype(_F32)
    lo = (x - hi - mid).astype(_BF16).astype(_F32)
    return hi, mid, lo


def kernel(x_prompt, x_sample, cache_k, cache_v, state_conv, page_table, g_attn, w_qkv, w_o,
           g_ffn, w_up, conv_w, conv_b, w_down, g_final):
    depth = w_qkv.shape[0]
    batch, seq, _ = x_prompt.shape
    dec_b, dec_s, _ = x_sample.shape
    n_pool = cache_k.shape[1]
    n_pages = page_table.shape[1]
    past = n_pages * PAGE_SIZE
    tp = batch * seq
    ts = dec_b * dec_s
    nb = seq // MOBA_BLOCK

    wqkv = w_qkv.astype(_BF16)
    wo = w_o.astype(_BF16)
    wup = w_up.astype(_BF16)
    cw = _chunk_cols(conv_w)
    cb = _chunk_cols(conv_b[:, None, :])
    wdn = w_down.astype(_BF16)
    g_attn2 = g_attn[:, None, :]
    g_ffn2 = g_ffn[:, None, :]
    g_fin2 = g_final[None, :]

    hidx = jnp.arange(1, N_HEADS + 1, dtype=_F32)
    slopes = jnp.exp2(-8.0 * hidx / N_HEADS)
    assert nb <= SLOPE_LANE0 - FLAG_LANE0 and past % MOBA_BLOCK == 0 and dec_s <= PAGE_SIZE
    assert seq % QKV_TM == 0 and seq % FFN_TM == 0 and n_pages % PAGES_PER_STEP == 0
    pieces = jnp.stack(_bf16_pieces(slopes * LOG2E) * 2, axis=1)
    stab = jnp.zeros((N_HEADS, 8, SLAB), _F32).at[:, :, SLOPE_LANE0:SLOPE_LANE0 + 6].set(pieces[:, None, :])
    slope_rows = jnp.repeat(slopes, dec_s)[:, None]
    tok_rows = jnp.tile(jnp.arange(dec_s, dtype=_F32), N_HEADS)[:, None]
    qpos_rows = tok_rows + float(past)
    tok_seq = jnp.tile(jnp.arange(dec_s, dtype=_F32), dec_b)[:, None]

    cache_kt = cache_k.transpose(0, 1, 3, 4, 2).reshape(depth, n_pool, KV_DIM, PAGE_SIZE)
    cache_vt = cache_v.transpose(0, 1, 3, 4, 2).reshape(depth, n_pool, KV_DIM, PAGE_SIZE)

    st = state_conv
    zero = jnp.zeros_like(st[:, :, :1])
    s1 = jnp.concatenate([st[:, :, 1:2], zero, zero, zero][:dec_s], axis=2)
    s2 = jnp.concatenate([st[:, :, 0:1], st[:, :, 1:2], zero, zero][:dec_s], axis=2)
    s1 = _chunk_cols(s1.reshape(depth, ts, 2 * D_FF))
    s2 = _chunk_cols(s2.reshape(depth, ts, 2 * D_FF))

    xp = x_prompt.reshape(tp, D_MODEL)
    xs = x_sample.reshape(ts, D_MODEL)
    ks_l, vs_l, cp_l, cs_l = [], [], [], []
    k_all = jnp.zeros((depth, batch, KV_DIM, seq), _F32)
    v_all = jnp.zeros((depth, batch, KV_DIM, seq), _F32)
    yp = ys = None
    pad_own = jnp.zeros((dec_b, PAGE_SIZE - dec_s, KV_DIM), _BF16)
    for l in range(depth):
        last = l == depth - 1
        gfin = g_fin2 if last else None
        moba = l % 2 == 0
        q, k_all, v_all, kslab, vslab, *km = _rms_qkv(xp, g_attn2[l], wqkv[l], tm=QKV_TM, with_kmean=moba,
                                                      kv_stack=(k_all, v_all), layer=l)
        if moba:
            kmt = km[0].reshape(batch, nb, N_KV_HEADS, HEAD_DIM).transpose(0, 2, 1, 3)
            kmt = jnp.pad(kmt, ((0, 0), (0, 0), (0, SLOPE_LANE0 - FLAG_LANE0 - nb),
                                (0, SLAB - HEAD_DIM)))
            knorm = jnp.sqrt(jnp.max(km[1].reshape(batch, nb, N_KV_HEADS, HEAD_DIM), axis=(1, 3)))
            o = _moba_prompt(knorm.reshape(batch * N_KV_HEADS), stab, q, kslab, vslab, kmt,
                             batch=batch, seq=seq)
        else:
            o = _stick_prompt(q, kslab, vslab, batch=batch, seq=seq)
        outs = _oproj_ffn(xp, o, wo[l], g_ffn2[l], wup[l], cw[l], cb[l], wdn[l],
                          tm=FFN_TM, seq=seq, g_final=gfin)
        xp, cst = outs[0], outs[1]
        if last:
            yp = outs[2]
        cp_l.append(_unchunk_cols(cst[:, :, 6:8, :]))
        qs, ksn, vsn, _, _ = _rms_qkv(xs, g_attn2[l], wqkv[l], tm=ts, with_kmean=False)
        qnat = qs[:, :, :HEAD_DIM].reshape(N_KV_HEADS, GROUP, dec_b, dec_s, HEAD_DIM)
        q_rows = qnat.transpose(2, 1, 3, 0, 4).reshape(dec_b, GROUP * dec_s, KV_DIM)
        k_own = jnp.concatenate([ksn.astype(_BF16).reshape(dec_b, dec_s, KV_DIM), pad_own], axis=1)
        v_own = jnp.concatenate([vsn.astype(_BF16).reshape(dec_b, dec_s, KV_DIM), pad_own], axis=1)
        if moba:
            o_rows = _moba_sample(page_table, slope_rows, qpos_rows, q_rows, k_own, v_own,
                                  cache_kt, cache_vt, layer=l)
        else:
            o_rows = _stick_sample(page_table, tok_rows, q_rows, k_own, v_own, cache_kt, cache_vt, layer=l)
        os_ = o_rows.reshape(dec_b, GROUP, dec_s, N_KV_HEADS, HEAD_DIM).transpose(0, 2, 3, 1, 4)
        outs = _oproj_ffn(xs, os_.reshape(ts, D_MODEL), wo[l], g_ffn2[l], wup[l], cw[l], cb[l], wdn[l],
                          tm=ts, seq=dec_s, sample_state=(s1[l], s2[l], tok_seq), g_final=gfin)
        xs, u_all = outs[0], outs[1]
        if last:
            ys = outs[2]
        ks_l.append(ksn.reshape(dec_b, dec_s, N_KV_HEADS, HEAD_DIM))
        vs_l.append(vsn.reshape(dec_b, dec_s, N_KV_HEADS, HEAD_DIM))
        u_full = _unchunk_cols(u_all).reshape(dec_b, dec_s, 2 * D_FF)
        cs_l.append(u_full[:, dec_s - (CONV_W - 1):, :])
    return (yp.reshape(batch, seq, D_MODEL), ys.reshape(dec_b, dec_s, D_MODEL),
            k_all.reshape(depth, batch, N_KV_HEADS, HEAD_DIM, seq).transpose(0, 1, 4, 2, 3),
            v_all.reshape(depth, batch, N_KV_HEADS, HEAD_DIM, seq).transpose(0, 1, 4, 2, 3),
            jnp.stack(ks_l), jnp.stack(vs_l),
            jnp.stack(cp_l), jnp.stack(cs_l))
```
